```python
import math
import jax, jax.numpy as jnp
from jax import lax
import numpy as np

D_MODEL = 1024
BATCH = 8
SEQ = 2048
DEPTH = 1

CHUNK = 64
CONV_DIM = D_MODEL // 2
CONV_WIDTH = 31
N_HEADS = 8
HEAD_DIM = 64
ATT_DIM = N_HEADS * HEAD_DIM
QBLOCK = 128
D_FF = ((8 * D_MODEL + 3 * 256 - 1) // (3 * 256)) * 256
EPS = 1e-6
IN_COLS = 2 * CONV_DIM + 3 * ATT_DIM + 2 * D_MODEL

kernel_name = "hybrid_conformer_conv_stickbreaking_block"


def rmsnorm(x, g):
    xf = x.astype(jnp.float32)
    y = xf * lax.rsqrt(jnp.mean(xf * xf, axis=-1, keepdims=True) + EPS)
    return (y * g.astype(jnp.float32)).astype(x.dtype)


def layernorm(x, g, b):
    xf = x.astype(jnp.float32)
    mu = jnp.mean(xf, axis=-1, keepdims=True)
    var = jnp.mean(jnp.square(xf - mu), axis=-1, keepdims=True)
    y = (xf - mu) * lax.rsqrt(var + EPS)
    return (y * g.astype(jnp.float32) + b.astype(jnp.float32)).astype(x.dtype)


def causal_depthwise_conv(u, w, b):
    k, c = w.shape
    up = jnp.pad(u, ((0, 0), (k - 1, 0), (0, 0)))
    y = lax.conv_general_dilated(
        up, w[:, None, :].astype(u.dtype), window_strides=(1,), padding='VALID',
        dimension_numbers=('NWC', 'WIO', 'NWC'), feature_group_count=c)
    return y + b.astype(u.dtype)


def stick_breaking_attention(q, k, v):
    s_len = q.shape[1]
    scale = 1.0 / math.sqrt(q.shape[-1])
    outs = []
    for i in range(s_len // QBLOCK):
        q0 = i * QBLOCK
        q1 = q0 + QBLOCK
        qb = q[:, q0:q1].astype(jnp.float32)
        kb = k[:, :q1].astype(jnp.float32)
        vb = v[:, :q1].astype(jnp.float32)
        z = jnp.einsum('bqhd,bkhd->bhqk', qb, kb) * scale
        t_idx = q0 + jnp.arange(QBLOCK)[:, None]
        s_idx = jnp.arange(q1)[None, :]
        mask = s_idx < t_idx
        log_1m = jnp.where(mask, jax.nn.log_sigmoid(-z), 0.0)
        rev_excl = lax.cumsum(log_1m, axis=3, reverse=True) - log_1m
        a = jnp.where(mask, jnp.exp(jax.nn.log_sigmoid(z) + rev_excl), 0.0)
        outs.append(jnp.einsum('bhqk,bkhd->bqhd', a, vb))
    return jnp.concatenate(outs, axis=1).astype(q.dtype)


def setup_inputs(seed: int = 0) -> dict:
    key = jax.random.key(seed)
    ks = jax.random.split(key, 20)
    f32 = jnp.float32

    def w(k, shape, fan_in):
        return jax.random.normal(k, shape, f32) * (fan_in ** -0.5)

    def gain(k, n):
        return 1.0 + 0.05 * jax.random.normal(k, (n,), f32)

    def bias(k, n):
        return 0.02 * jax.random.normal(k, (n,), f32)

    return {
        'x': jax.random.normal(ks[0], (BATCH, SEQ, D_MODEL), f32),
        'norm_mix_pre': gain(ks[1], D_MODEL),
        'w_in': w(ks[2], (D_MODEL, IN_COLS), D_MODEL),
        'conv_dw_w': w(ks[3], (CONV_WIDTH, CONV_DIM), CONV_WIDTH),
        'conv_dw_b': bias(ks[4], CONV_DIM),
        'conv_ln_g': gain(ks[5], CONV_DIM),
        'conv_ln_b': bias(ks[6], CONV_DIM),
        'w_conv_branch': w(ks[7], (CONV_DIM, D_MODEL), CONV_DIM),
        'b_conv_branch': bias(ks[8], D_MODEL),
        'w_att_branch': w(ks[9], (ATT_DIM, D_MODEL), ATT_DIM),
        'w_out': w(ks[10], (D_MODEL, D_MODEL), D_MODEL),
        'norm_mix_post': gain(ks[11], D_MODEL),
        'norm_ffn_pre': gain(ks[12], D_MODEL),
        'w_ffn_up': w(ks[13], (D_MODEL, 2 * D_FF), D_MODEL),
        'w_ffn_down': w(ks[14], (D_FF, D_MODEL), D_FF),
        'norm_ffn_post': gain(ks[15], D_MODEL),
    }


def reference(x, norm_mix_pre, w_in, conv_dw_w, conv_dw_b, conv_ln_g, conv_ln_b,
              w_conv_branch, b_conv_branch, w_att_branch, w_out, norm_mix_post,
              norm_ffn_pre, w_ffn_up, w_ffn_down, norm_ffn_post):
    b, s, d = x.shape
    for _ in range(DEPTH):
        h = rmsnorm(x, norm_mix_pre)
        proj = jnp.einsum('bsd,de->bse', h, w_in)
        splits = np.cumsum([2 * CONV_DIM, ATT_DIM, ATT_DIM, ATT_DIM, D_MODEL]).tolist()
        conv_in, q, k, v, g_conv, g_att = jnp.split(proj, splits, axis=-1)

        u = jax.nn.glu(conv_in, axis=-1)
        u = causal_depthwise_conv(u, conv_dw_w, conv_dw_b)
        u = jax.nn.silu(layernorm(u, conv_ln_g, conv_ln_b))
        conv_out = jnp.einsum('bsc,cd->bsd', u, w_conv_branch) + b_conv_branch

        q = q.reshape(b, s, N_HEADS, HEAD_DIM)
        k = k.reshape(b, s, N_HEADS, HEAD_DIM)
        v = v.reshape(b, s, N_HEADS, HEAD_DIM)
        att = stick_breaking_attention(q, k, v).reshape(b, s, ATT_DIM)
        att_out = jnp.einsum('bsc,cd->bsd', att, w_att_branch)

        merged = jax.nn.sigmoid(g_conv) * conv_out + jax.nn.sigmoid(g_att) * att_out
        mix = jnp.einsum('bsd,de->bse', merged, w_out)
        x = x + rmsnorm(mix, norm_mix_post)

        h = rmsnorm(x, norm_ffn_pre)
        gu = jnp.einsum('bsd,df->bsf', h, w_ffn_up)
        gate, up = jnp.split(gu, 2, axis=-1)
        ff = jnp.einsum('bsf,fd->bsd', jax.nn.silu(gate) * up, w_ffn_down)
        x = x + rmsnorm(ff, norm_ffn_post)
    return x
```

```python
import functools

import jax
import jax.numpy as jnp
from jax import lax
from jax.experimental import pallas as pl
from jax.experimental.pallas import tpu as pltpu

F32 = jnp.float32
BF16 = jnp.bfloat16

CONV_WIDTH = 31
HEAD_DIM = 64
EPS = 1e-6

SUBLANES = 8
MXU_DIM = 256
VMEM_LIMIT_BYTES = 56 * 1024 * 1024

TM_PROJ = 512
TM_MIX = 256
TM_FFN = 512
TQ = 256
HEADS_PER_GROUP = MXU_DIM // HEAD_DIM
HALO = 32
CONV_ROWS = 64


def _const_spec(shape):
    return pl.BlockSpec(shape, lambda *_: (0,) * len(shape))


def _rms(x, g):
    return x * lax.rsqrt(jnp.mean(x * x, axis=-1, keepdims=True) + EPS) * g


def _inproj_kernel(x_ref, g_ref, w_ref, u_ref, q_ref, k_ref, v_ref, gc_ref, ga_ref, *, conv_dim, att_dim, d_model):
    h = _rms(x_ref[...], g_ref[...]).astype(BF16)

    def proj(c0, width):
        return jnp.dot(h, w_ref[:, c0:c0 + width], preferred_element_type=F32)

    c = 0
    a = proj(c, conv_dim)
    b = proj(c + conv_dim, conv_dim)
    u_ref[...] = a * jax.nn.sigmoid(b)
    c += 2 * conv_dim
    q_ref[...] = (proj(c, att_dim) * (HEAD_DIM ** -0.5)).astype(BF16)
    k_ref[...] = proj(c + att_dim, att_dim).astype(BF16)
    v_ref[...] = proj(c + 2 * att_dim, att_dim).astype(BF16)
    c += 3 * att_dim
    step = 512
    for o in range(0, d_model, step):
        gc_ref[:, o:o + step] = jax.nn.sigmoid(proj(c + o, step)).astype(BF16)
    c += d_model
    for o in range(0, d_model, step):
        ga_ref[:, o:o + step] = jax.nn.sigmoid(proj(c + o, step)).astype(BF16)


def _inproj(x2, g, w_in, conv_dim, att_dim):
    n, d = x2.shape
    in_cols = w_in.shape[1]
    row = lambda width: pl.BlockSpec((TM_PROJ, width), lambda i: (i, 0))
    return pl.pallas_call(
        functools.partial(_inproj_kernel, conv_dim=conv_dim, att_dim=att_dim, d_model=d),
        grid=(n // TM_PROJ,),
        in_specs=[row(d), _const_spec((1, d)), _const_spec((d, in_cols))],
        out_specs=[row(conv_dim), row(att_dim), row(att_dim), row(att_dim), row(d), row(d)],
        out_shape=[
            jax.ShapeDtypeStruct((n, conv_dim), F32),
            jax.ShapeDtypeStruct((n, att_dim), BF16),
            jax.ShapeDtypeStruct((n, att_dim), BF16),
            jax.ShapeDtypeStruct((n, att_dim), BF16),
            jax.ShapeDtypeStruct((n, d), BF16),
            jax.ShapeDtypeStruct((n, d), BF16),
        ],
        compiler_params=pltpu.CompilerParams(
            dimension_semantics=("arbitrary",), vmem_limit_bytes=VMEM_LIMIT_BYTES),
        name="inproj",
    )(x2, g, w_in)


def _attn_kernel(q_ref, k_ref, v_ref, o_ref, acc_ref):
    i = pl.program_id(2)
    lane_head = lax.broadcasted_iota(jnp.int32, (1, MXU_DIM), 1) // HEAD_DIM
    head_masks = [lane_head == hh for hh in range(HEADS_PER_GROUP)]
    q = q_ref[...]
    q_heads = [jnp.where(m, q, jnp.zeros_like(q)) for m in head_masks]

    key_idx = lax.broadcasted_iota(jnp.int32, (TQ, TQ), 0)
    col_idx = lax.broadcasted_iota(jnp.int32, (TQ, TQ), 1)
    suffix_ones = (key_idx >= col_idx).astype(BF16)
    strictly_causal = col_idx < key_idx

    acc_ref[...] = jnp.zeros_like(acc_ref)

    def block(k0, diag, carry):
        kb = k_ref[pl.ds(k0, TQ), :]
        vb = v_ref[pl.ds(k0, TQ), :]
        new_carry = []
        acc = None
        for hh in range(HEADS_PER_GROUP):
            z = lax.dot_general(q_heads[hh], kb, (((1,), (1,)), ((), ())), preferred_element_type=F32)
            sp = jnp.maximum(z, 0.0) + jnp.log(1.0 + jnp.exp(-jnp.abs(z)))
            if diag:
                sp = jnp.where(strictly_causal, sp, 0.0)
            suffix = jnp.dot(sp.astype(BF16), suffix_ones, preferred_element_type=F32)
            a = jnp.exp(z - suffix + carry[hh])
            if diag:
                a = jnp.where(strictly_causal, a, 0.0)
            vm = jnp.where(head_masks[hh], vb, jnp.zeros_like(vb))
            pv = jnp.dot(a.astype(BF16), vm, preferred_element_type=F32)
            acc = pv if acc is None else acc + pv
            new_carry.append(carry[hh] - suffix[:, 0:1])
        acc_ref[...] += acc
        return tuple(new_carry)

    zero = jnp.zeros((TQ, 1), F32)
    carry = block(pl.multiple_of(i * TQ, TQ), True, (zero,) * HEADS_PER_GROUP)

    def body(step, carry):
        return block(pl.multiple_of((i - 1 - step) * TQ, TQ), False, carry)

    lax.fori_loop(0, i, body, carry)
    o_ref[...] = acc_ref[...].astype(o_ref.dtype)


def _attention(q, k, v, batch, seq):
    n, att_dim = q.shape
    groups = att_dim // MXU_DIM
    qblocks = seq // TQ
    qspec = pl.BlockSpec((TQ, MXU_DIM), lambda b, g, i: (b * qblocks + i, g))
    kvspec = pl.BlockSpec((seq, MXU_DIM), lambda b, g, i: (b, g))
    return pl.pallas_call(
        _attn_kernel,
        grid=(batch, groups, qblocks),
        in_specs=[qspec, kvspec, kvspec],
        out_specs=qspec,
        out_shape=jax.ShapeDtypeStruct((n, att_dim), BF16),
        scratch_shapes=[pltpu.VMEM((TQ, MXU_DIM), F32)],
        compiler_params=pltpu.CompilerParams(
            dimension_semantics=("arbitrary", "arbitrary", "arbitrary"), vmem_limit_bytes=VMEM_LIMIT_BYTES),
        name="stickbreak_attn",
    )(q, k, v)


def _mix_kernel(u_ref, uprev_ref, att_ref, gc_ref, ga_ref, x_ref, cw_ref, cb_ref, lng_ref, lnb_ref,
                wc_ref, bc_ref, wa_ref, wo_ref, gpost_ref, o_ref, ext_ref, conv_ref, *, tiles_per_seq):
    first = (pl.program_id(0) % tiles_per_seq) == 0
    ext_ref[0:HALO, :] = jnp.where(first, 0.0, uprev_ref[...])
    ext_ref[HALO:HALO + TM_MIX, :] = u_ref[...]

    lead = HALO - (CONV_WIDTH - 1)
    taps = [cw_ref[j:j + 1, :] for j in range(CONV_WIDTH)]
    for r0 in range(0, TM_MIX, CONV_ROWS):
        acc = jnp.broadcast_to(cb_ref[...], (CONV_ROWS, cb_ref.shape[1]))
        for phase in range(SUBLANES):
            offs = [o for o in range(lead, lead + CONV_WIDTH) if o % SUBLANES == phase]
            base = offs[0]
            span = offs[-1] - base + CONV_ROWS
            shifted = ext_ref[r0 + base:r0 + base + span, :]
            for o in offs:
                acc = acc + taps[o - lead] * shifted[o - base:o - base + CONV_ROWS, :]
        conv_ref[r0:r0 + CONV_ROWS, :] = acc

    y = conv_ref[...]
    mu = jnp.mean(y, axis=-1, keepdims=True)
    yc = y - mu
    var = jnp.mean(yc * yc, axis=-1, keepdims=True)
    y = yc * lax.rsqrt(var + EPS) * lng_ref[...] + lnb_ref[...]
    y = y * jax.nn.sigmoid(y)
    conv_out = jnp.dot(y.astype(BF16), wc_ref[...], preferred_element_type=F32) + bc_ref[...]
    att_out = jnp.dot(att_ref[...], wa_ref[...], preferred_element_type=F32)
    merged = gc_ref[...].astype(F32) * conv_out + ga_ref[...].astype(F32) * att_out
    mix = jnp.dot(merged.astype(BF16), wo_ref[...], preferred_element_type=F32)
    o_ref[...] = x_ref[...] + _rms(mix, gpost_ref[...])


def _mix(u, att, gc, ga, x2, cw, cb, lng, lnb, wc, bc, wa, wo, gpost, seq):
    n, d = x2.shape
    conv_dim = u.shape[1]
    att_dim = att.shape[1]
    row = lambda width: pl.BlockSpec((TM_MIX, width), lambda i: (i, 0))
    halo_per_tile = TM_MIX // HALO
    prev = pl.BlockSpec((HALO, conv_dim), lambda i: (jnp.maximum(i * halo_per_tile - 1, 0), 0))
    return pl.pallas_call(
        functools.partial(_mix_kernel, tiles_per_seq=seq // TM_MIX),
        grid=(n // TM_MIX,),
        in_specs=[row(conv_dim), prev, row(att_dim), row(d), row(d), row(d),
                  _const_spec(cw.shape), _const_spec(cb.shape), _const_spec(lng.shape), _const_spec(lnb.shape),
                  _const_spec(wc.shape), _const_spec(bc.shape), _const_spec(wa.shape), _const_spec(wo.shape),
                  _const_spec(gpost.shape)],
        out_specs=row(d),
        out_shape=jax.ShapeDtypeStruct((n, d), F32),
        scratch_shapes=[pltpu.VMEM((HALO + TM_MIX, conv_dim), F32), pltpu.VMEM((TM_MIX, conv_dim), F32)],
        compiler_params=pltpu.CompilerParams(
            dimension_semantics=("arbitrary",), vmem_limit_bytes=VMEM_LIMIT_BYTES),
        name="mix_merge",
    )(u, u, att, gc, ga, x2, cw, cb, lng, lnb, wc, bc, wa, wo, gpost)


def _ffn_kernel(x_ref, gpre_ref, wup_ref, wdn_ref, gpost_ref, o_ref, *, d_ff, chunk):
    x = x_ref[...]
    h = _rms(x, gpre_ref[...]).astype(BF16)
    ff = None
    for c0 in range(0, d_ff, chunk):
        width = min(chunk, d_ff - c0)
        gate = jnp.dot(h, wup_ref[:, c0:c0 + width], preferred_element_type=F32)
        up = jnp.dot(h, wup_ref[:, d_ff + c0:d_ff + c0 + width], preferred_element_type=F32)
        act = (gate * jax.nn.sigmoid(gate) * up).astype(BF16)
        part = jnp.dot(act, wdn_ref[c0:c0 + width, :], preferred_element_type=F32)
        ff = part if ff is None else ff + part
    o_ref[...] = x + _rms(ff, gpost_ref[...])


def _ffn(x1, gpre, wup, wdn, gpost):
    n, d = x1.shape
    d_ff = wdn.shape[0]
    row = pl.BlockSpec((TM_FFN, d), lambda i: (i, 0))
    return pl.pallas_call(
        functools.partial(_ffn_kernel, d_ff=d_ff, chunk=512),
        grid=(n // TM_FFN,),
        in_specs=[row, _const_spec(gpre.shape), _const_spec(wup.shape), _const_spec(wdn.shape),
                  _const_spec(gpost.shape)],
        out_specs=row,
        out_shape=jax.ShapeDtypeStruct((n, d), F32),
        compiler_params=pltpu.CompilerParams(
            dimension_semantics=("arbitrary",), vmem_limit_bytes=VMEM_LIMIT_BYTES),
        name="ffn",
    )(x1, gpre, wup, wdn, gpost)


def kernel(x, norm_mix_pre, w_in, conv_dw_w, conv_dw_b, conv_ln_g, conv_ln_b, w_conv_branch, b_conv_branch,
           w_att_branch, w_out, norm_mix_post, norm_ffn_pre, w_ffn_up, w_ffn_down, norm_ffn_post):
    b, s, d = x.shape
    conv_dim = conv_dw_w.shape[1]
    att_dim = w_att_branch.shape[0]
    assert att_dim % MXU_DIM == 0 and s % TQ == 0 and (b * s) % TM_PROJ == 0 and s % TM_MIX == 0
    assert conv_dw_w.shape[0] == CONV_WIDTH and w_in.shape[1] == 2 * conv_dim + 3 * att_dim + 2 * d
    rowvec = lambda a: a.reshape(1, -1).astype(F32)

    x2 = x.reshape(b * s, d)
    u, q, k, v, gc, ga = _inproj(x2, rowvec(norm_mix_pre), w_in.astype(BF16), conv_dim, att_dim)
    att = _attention(q, k, v, b, s)
    x1 = _mix(u, att, gc, ga, x2, conv_dw_w.astype(F32), rowvec(conv_dw_b), rowvec(conv_ln_g), rowvec(conv_ln_b),
              w_conv_branch.astype(BF16), rowvec(b_conv_branch), w_att_branch.astype(BF16), w_out.astype(BF16),
              rowvec(norm_mix_post), s)
    out = _ffn(x1, rowvec(norm_ffn_pre), w_ffn_up.astype(BF16), w_ffn_down.astype(BF16), rowvec(norm_ffn_post))
    return out.reshape(b, s, d)
```

```python
import functools

import jax
import jax.numpy as jnp
from jax import lax
from jax.experimental import pallas as pl
from jax.experimental.pallas import tpu as pltpu

F32 = jnp.float32
BF16 = jnp.bfloat16

CONV_WIDTH = 31
HEAD_DIM = 64
EPS = 1e-6

SUBLANES = 8
LANES = 128
MXU_DIM = 256
VMEM_LIMIT_BYTES = 56 * 1024 * 1024

TM_PROJ = 512
TM_MIX = 256
TM_FFN = 512
TQ = 256
HEADS_PER_GROUP = MXU_DIM // HEAD_DIM
SLAB_ROWS = 2 * TQ
HALO = 32
CONV_ROWS = 64
LOG2E = 1.4426950408889634
EXIT_LOG2 = 160.0
SOFTPLUS_CLAMP = 120.0


def _const_spec(shape):
    return pl.BlockSpec(shape, lambda *_: (0,) * len(shape))


def _rms(x, g):
    return x * lax.rsqrt(jnp.mean(x * x, axis=-1, keepdims=True) + EPS) * g


def _inproj_kernel(x_ref, g_ref, w_ref, u_ref, q_ref, k_ref, v_ref, gc_ref, ga_ref, *, conv_dim, att_dim, d_model):
    h = _rms(x_ref[...], g_ref[...]).astype(BF16)

    def proj(c0, width):
        return jnp.dot(h, w_ref[:, c0:c0 + width], preferred_element_type=F32)

    c = 0
    a = proj(c, conv_dim)
    b = proj(c + conv_dim, conv_dim)
    u_ref[...] = a * jax.nn.sigmoid(b)
    c += 2 * conv_dim
    q_ref[...] = (proj(c, att_dim) * (LOG2E * HEAD_DIM ** -0.5)).astype(BF16)
    k_ref[...] = proj(c + att_dim, att_dim).astype(BF16)
    v_ref[...] = proj(c + 2 * att_dim, att_dim).astype(BF16)
    c += 3 * att_dim
    step = 512
    for o in range(0, d_model, step):
        gc_ref[:, o:o + step] = jax.nn.sigmoid(proj(c + o, step)).astype(BF16)
    c += d_model
    for o in range(0, d_model, step):
        ga_ref[:, o:o + step] = jax.nn.sigmoid(proj(c + o, step)).astype(BF16)


def _inproj(x2, g, w_in, conv_dim, att_dim):
    n, d = x2.shape
    in_cols = w_in.shape[1]
    row = lambda width: pl.BlockSpec((TM_PROJ, width), lambda i: (i, 0))
    return pl.pallas_call(
        functools.partial(_inproj_kernel, conv_dim=conv_dim, att_dim=att_dim, d_model=d),
        grid=(n // TM_PROJ,),
        in_specs=[row(d), _const_spec((1, d)), _const_spec((d, in_cols))],
        out_specs=[row(conv_dim), row(att_dim), row(att_dim), row(att_dim), row(d), row(d)],
        out_shape=[
            jax.ShapeDtypeStruct((n, conv_dim), F32),
            jax.ShapeDtypeStruct((n, att_dim), BF16),
            jax.ShapeDtypeStruct((n, att_dim), BF16),
            jax.ShapeDtypeStruct((n, att_dim), BF16),
            jax.ShapeDtypeStruct((n, d), BF16),
            jax.ShapeDtypeStruct((n, d), BF16),
        ],
        compiler_params=pltpu.CompilerParams(
            dimension_semantics=("arbitrary",), vmem_limit_bytes=VMEM_LIMIT_BYTES),
        name="inproj",
    )(x2, g, w_in)


def _softplus2(z):
    return jnp.maximum(jnp.log2(1.0 + jnp.exp2(jnp.minimum(z, SOFTPLUS_CLAMP))), z)


def _attn_kernel(q_ref, k_ref, v_ref, o_ref, acc_ref, carry_ref, qh_ref, vh_ref, live_ref):
    i = pl.program_id(2)
    lane_head = lax.broadcasted_iota(jnp.int32, (1, MXU_DIM), 1) // HEAD_DIM
    head_masks = [lane_head == hh for hh in range(HEADS_PER_GROUP)]

    @pl.when(i == 0)
    def _():
        for hh in range(HEADS_PER_GROUP):
            for r0 in range(0, vh_ref.shape[1], TQ):
                vb = v_ref[r0:r0 + TQ, :]
                vh_ref[hh, r0:r0 + TQ, :] = jnp.where(head_masks[hh], vb, jnp.zeros_like(vb))

    q = q_ref[...]
    for hh in range(HEADS_PER_GROUP):
        qh_ref[hh * TQ:(hh + 1) * TQ, :] = jnp.where(head_masks[hh], q, jnp.zeros_like(q))

    rows = HEADS_PER_GROUP * TQ
    key_idx = lax.broadcasted_iota(jnp.int32, (TQ, TQ), 0)
    col_idx = lax.broadcasted_iota(jnp.int32, (TQ, TQ), 1)
    suffix_ones = (key_idx >= col_idx).astype(BF16)
    t_idx = lax.broadcasted_iota(jnp.int32, (rows, TQ), 0) % TQ
    strictly_causal = lax.broadcasted_iota(jnp.int32, (rows, TQ), 1) < t_idx

    acc_ref[...] = jnp.zeros_like(acc_ref)
    carry_ref[...] = jnp.zeros_like(carry_ref)

    def block(k0, diag):
        kb = k_ref[pl.ds(k0, TQ), :]
        acc = acc_ref[...]
        slowest = None
        for r0 in range(0, rows, SLAB_ROWS):
            sl = slice(r0, r0 + SLAB_ROWS)
            z = lax.dot_general(qh_ref[sl, :], kb, (((1,), (1,)), ((), ())), preferred_element_type=F32)
            sp = _softplus2(z)
            if diag:
                sp = jnp.where(strictly_causal[sl], sp, 0.0)
            suffix = jnp.dot(sp.astype(BF16), suffix_ones, preferred_element_type=F32)
            carry = carry_ref[sl, :]
            a = jnp.exp2(z - suffix + jnp.concatenate([carry] * (TQ // LANES), axis=1))
            if diag:
                a = jnp.where(strictly_causal[sl], a, 0.0)
            a = a.astype(BF16)
            for h0 in range(0, SLAB_ROWS, TQ):
                acc += jnp.dot(a[h0:h0 + TQ, :], vh_ref[(r0 + h0) // TQ, pl.ds(k0, TQ), :],
                               preferred_element_type=F32)
            carry = carry - jnp.broadcast_to(suffix[:, 0:1], carry.shape)
            carry_ref[sl, :] = carry
            slowest = carry if slowest is None else jnp.maximum(slowest, carry)
        acc_ref[...] = acc
        return (jnp.max(slowest) > -EXIT_LOG2).astype(jnp.int32)

    def key_start(back):
        return pl.multiple_of((i - back) * TQ, TQ)

    @pl.when(i == 0)
    def _():
        live_ref[0] = block(key_start(0), True)

    @pl.when(i > 0)
    def _():
        block(key_start(0), True)
        live_ref[0] = block(key_start(1), False)

    def cond(state):
        back, live = state
        return jnp.logical_and(back <= i, live > 0)

    def body(state):
        back, _ = state
        return back + 1, block(key_start(back), False)

    lax.while_loop(cond, body, (jnp.int32(2), live_ref[0]))
    o_ref[...] = acc_ref[...].astype(o_ref.dtype)


def _attention(q, k, v, batch, seq):
    n, att_dim = q.shape
    groups = att_dim // MXU_DIM
    qblocks = seq // TQ
    qspec = pl.BlockSpec((TQ, MXU_DIM), lambda b, g, i: (b * qblocks + i, g))
    kvspec = pl.BlockSpec((seq, MXU_DIM), lambda b, g, i: (b, g))
    return pl.pallas_call(
        _attn_kernel,
        grid=(batch, groups, qblocks),
        in_specs=[qspec, kvspec, kvspec],
        out_specs=qspec,
        out_shape=jax.ShapeDtypeStruct((n, att_dim), BF16),
        scratch_shapes=[pltpu.VMEM((TQ, MXU_DIM), F32),
                        pltpu.VMEM((HEADS_PER_GROUP * TQ, LANES), F32),
                        pltpu.VMEM((HEADS_PER_GROUP * TQ, MXU_DIM), BF16),
                        pltpu.VMEM((HEADS_PER_GROUP, seq, MXU_DIM), BF16),
                        pltpu.SMEM((1,), jnp.int32)],
        compiler_params=pltpu.CompilerParams(
            dimension_semantics=("arbitrary", "arbitrary", "arbitrary"), vmem_limit_bytes=VMEM_LIMIT_BYTES),
        name="stickbreak_attn",
    )(q, k, v)


def _mix_kernel(u_ref, uprev_ref, att_ref, gc_ref, ga_ref, x_ref, cw_ref, cb_ref, lng_ref, lnb_ref,
                wc_ref, bc_ref, wa_ref, wo_ref, gpost_ref, o_ref, ext_ref, conv_ref, *, tiles_per_seq):
    first = (pl.program_id(0) % tiles_per_seq) == 0
    ext_ref[0:HALO, :] = jnp.where(first, 0.0, uprev_ref[...])
    ext_ref[HALO:HALO + TM_MIX, :] = u_ref[...]

    taps = [cw_ref[j:j + 1, :] for j in range(CONV_WIDTH)]
    rows = CONV_ROWS + SUBLANES
    for r0 in range(0, TM_MIX, CONV_ROWS):
        acc = jnp.broadcast_to(cb_ref[...], (CONV_ROWS, cb_ref.shape[1]))
        for r in range(SUBLANES):
            part = None
            for d in range(r, CONV_WIDTH, SUBLANES):
                start = r0 + HALO - SUBLANES - (d - r)
                term = taps[CONV_WIDTH - 1 - d] * ext_ref[start:start + rows, :]
                part = term if part is None else part + term
            if r:
                part = pltpu.roll(part, r, axis=0)
            acc = acc + part[SUBLANES:, :]
        conv_ref[r0:r0 + CONV_ROWS, :] = acc

    y = conv_ref[...]
    mu = jnp.mean(y, axis=-1, keepdims=True)
    yc = y - mu
    var = jnp.mean(yc * yc, axis=-1, keepdims=True)
    y = yc * lax.rsqrt(var + EPS) * lng_ref[...] + lnb_ref[...]
    y = y * jax.nn.sigmoid(y)
    conv_out = jnp.dot(y.astype(BF16), wc_ref[...], preferred_element_type=F32) + bc_ref[...]
    att_out = jnp.dot(att_ref[...], wa_ref[...], preferred_element_type=F32)
    merged = gc_ref[...].astype(F32) * conv_out + ga_ref[...].astype(F32) * att_out
    mix = jnp.dot(merged.astype(BF16), wo_ref[...], preferred_element_type=F32)
    o_ref[...] = x_ref[...] + _rms(mix, gpost_ref[...])


def _mix(u, att, gc, ga, x2, cw, cb, lng, lnb, wc, bc, wa, wo, gpost, seq):
    n, d = x2.shape
    conv_dim = u.shape[1]
    att_dim = att.shape[1]
    row = lambda width: pl.BlockSpec((TM_MIX, width), lambda i: (i, 0))
    halo_per_tile = TM_MIX // HALO
    prev = pl.BlockSpec((HALO, conv_dim), lambda i: (jnp.maximum(i * halo_per_tile - 1, 0), 0))
    return pl.pallas_call(
        functools.partial(_mix_kernel, tiles_per_seq=seq // TM_MIX),
        grid=(n // TM_MIX,),
        in_specs=[row(conv_dim), prev, row(att_dim), row(d), row(d), row(d),
                  _const_spec(cw.shape), _const_spec(cb.shape), _const_spec(lng.shape), _const_spec(lnb.shape),
                  _const_spec(wc.shape), _const_spec(bc.shape), _const_spec(wa.shape), _const_spec(wo.shape),
                  _const_spec(gpost.shape)],
        out_specs=row(d),
        out_shape=jax.ShapeDtypeStruct((n, d), F32),
        scratch_shapes=[pltpu.VMEM((HALO + TM_MIX, conv_dim), F32), pltpu.VMEM((TM_MIX, conv_dim), F32)],
        compiler_params=pltpu.CompilerParams(
            dimension_semantics=("arbitrary",), vmem_limit_bytes=VMEM_LIMIT_BYTES),
        name="mix_merge",
    )(u, u, att, gc, ga, x2, cw, cb, lng, lnb, wc, bc, wa, wo, gpost)


def _ffn_kernel(x_ref, gpre_ref, wup_ref, wdn_ref, gpost_ref, o_ref, *, d_ff, chunk):
    x = x_ref[...]
    h = _rms(x, gpre_ref[...]).astype(BF16)
    ff = None
    for c0 in range(0, d_ff, chunk):
        width = min(chunk, d_ff - c0)
        gate = jnp.dot(h, wup_ref[:, c0:c0 + width], preferred_element_type=F32)
        up = jnp.dot(h, wup_ref[:, d_ff + c0:d_ff + c0 + width], preferred_element_type=F32)
        act = (gate * jax.nn.sigmoid(gate) * up).astype(BF16)
        part = jnp.dot(act, wdn_ref[c0:c0 + width, :], preferred_element_type=F32)
        ff = part if ff is None else ff + part
    o_ref[...] = x + _rms(ff, gpost_ref[...])


def _ffn(x1, gpre, wup, wdn, gpost):
    n, d = x1.shape
    d_ff = wdn.shape[0]
    row = pl.BlockSpec((TM_FFN, d), lambda i: (i, 0))
    return pl.pallas_call(
        functools.partial(_ffn_kernel, d_ff=d_ff, chunk=512),
        grid=(n // TM_FFN,),
        in_specs=[row, _const_spec(gpre.shape), _const_spec(wup.shape), _const_spec(wdn.shape),
                  _const_spec(gpost.shape)],
        out_specs=row,
        out_shape=jax.ShapeDtypeStruct((n, d), F32),
        compiler_params=pltpu.CompilerParams(
            dimension_semantics=("arbitrary",), vmem_limit_bytes=VMEM_LIMIT_BYTES),
        name="ffn",
    )(x1, gpre, wup, wdn, gpost)


def kernel(x, norm_mix_pre, w_in, conv_dw_w, conv_dw_b, conv_ln_g, conv_ln_b, w_conv_branch, b_conv_branch,
           w_att_branch, w_out, norm_mix_post, norm_ffn_pre, w_ffn_up, w_ffn_down, norm_ffn_post):
    b, s, d = x.shape
    conv_dim = conv_dw_w.shape[1]
    att_dim = w_att_branch.shape[0]
    assert att_dim % MXU_DIM == 0 and s % TQ == 0 and (b * s) % TM_PROJ == 0 and s % TM_MIX == 0
    assert conv_dw_w.shape[0] == CONV_WIDTH and w_in.shape[1] == 2 * conv_dim + 3 * att_dim + 2 * d
    assert HALO >= CONV_WIDTH - 1 - (CONV_WIDTH - 1) % SUBLANES + SUBLANES
    rowvec = lambda a: a.reshape(1, -1).astype(F32)

    x2 = x.reshape(b * s, d)
    u, q, k, v, gc, ga = _inproj(x2, rowvec(norm_mix_pre), w_in.astype(BF16), conv_dim, att_dim)
    att = _attention(q, k, v, b, s)
    x1 = _mix(u, att, gc, ga, x2, conv_dw_w.astype(F32), rowvec(conv_dw_b), rowvec(conv_ln_g), rowvec(conv_ln_b),
              w_conv_branch.astype(BF16), rowvec(b_conv_branch), w_att_branch.astype(BF16), w_out.astype(BF16),
              rowvec(norm_mix_post), s)
    out = _ffn(x1, rowvec(norm_ffn_pre), w_ffn_up.astype(BF16), w_ffn_down.astype(BF16), rowvec(norm_ffn_post))
    return out.reshape(b, s, d)
```

```python
import functools

import jax
import jax.numpy as jnp
from jax import lax
from jax.experimental import pallas as pl
from jax.experimental.pallas import tpu as pltpu

F32 = jnp.float32
BF16 = jnp.bfloat16

CONV_WIDTH = 31
HEAD_DIM = 64
EPS = 1e-6

SUBLANES = 8
LANES = 128
MXU_DIM = 256
VMEM_LIMIT_BYTES = 56 * 1024 * 1024

TM_PROJ = 512
TM_FFN = 512
TQ = 256
HEADS_PER_GROUP = MXU_DIM // HEAD_DIM
SLAB_ROWS = 2 * TQ
HALO = 32
CONV_ROWS = 64
LOG2E = 1.4426950408889634
EXIT_LOG2 = 160.0
SOFTPLUS_CLAMP = 120.0


def _const_spec(shape):
    return pl.BlockSpec(shape, lambda *_: (0,) * len(shape), pipeline_mode=pl.Buffered(1))


def _rms(x, g):
    return x * lax.rsqrt(jnp.mean(x * x, axis=-1, keepdims=True) + EPS) * g


def _exact_zero_like(x):
    bits = lax.bitcast_convert_type(x, jnp.int32)
    return lax.shift_right_logical(lax.shift_right_logical(bits, 16), 16).astype(F32)


def _conv_chunk(ext_ref, taps, bias, r0):
    rows = CONV_ROWS + SUBLANES
    acc = jnp.broadcast_to(bias, (CONV_ROWS, bias.shape[1]))
    for r in range(SUBLANES):
        part = None
        for d in range(r, CONV_WIDTH, SUBLANES):
            start = r0 + HALO - SUBLANES - (d - r)
            term = taps[CONV_WIDTH - 1 - d] * ext_ref[start:start + rows, :]
            part = term if part is None else part + term
        if r:
            part = pltpu.roll(part, r, axis=0)
        acc = acc + part[SUBLANES:, :]
    return acc


def _inproj_kernel(x_ref, g_ref, w_ref, cw_ref, cb_ref,
                   y_ref, q_ref, k_ref, v_ref, gc_ref, ga_ref, ext_ref,
                   *, conv_dim, att_dim, d_model, tiles_per_seq):
    h = _rms(x_ref[...], g_ref[...]).astype(BF16)

    def proj(c0, width):
        return jnp.dot(h, w_ref[:, c0:c0 + width], preferred_element_type=F32)

    first = (pl.program_id(0) % tiles_per_seq) == 0

    @pl.when(first)
    def _():
        ext_ref[0:HALO, :] = jnp.zeros((HALO, conv_dim), F32)

    @pl.when(jnp.logical_not(first))
    def _():
        ext_ref[0:HALO, :] = ext_ref[TM_PROJ:TM_PROJ + HALO, :]

    ext_ref[HALO:HALO + TM_PROJ, :] = proj(0, conv_dim) * jax.nn.sigmoid(proj(conv_dim, conv_dim))

    c = 2 * conv_dim
    piece = 512
    pieces = []
    for ref, scale in ((q_ref, LOG2E * HEAD_DIM ** -0.5), (k_ref, None), (v_ref, None)):
        for o in range(0, att_dim, piece):
            pieces.append((ref, o, c + o, scale, False))
        c += att_dim
    for ref in (gc_ref, ga_ref):
        for o in range(0, d_model, piece):
            pieces.append((ref, o, c + o, None, True))
        c += d_model

    def run_piece(ref, o, col, scale, gate):
        p = proj(col, piece)
        if scale is not None:
            p = p * scale
        if gate:
            p = jax.nn.sigmoid(p)
        ref[:, o:o + piece] = p.astype(BF16)
        return _exact_zero_like(p[0:1, 0:conv_dim])

    taps = [cw_ref[j:j + 1, :] for j in range(CONV_WIDTH)]
    bias = cb_ref[...]
    chunks = list(range(0, TM_PROJ, CONV_ROWS))
    for idx, r0 in enumerate(chunks):
        y_ref[r0:r0 + CONV_ROWS, :] = _conv_chunk(ext_ref, taps, bias, r0)
        lo, hi = idx * len(pieces) // (len(chunks) - 1), (idx + 1) * len(pieces) // (len(chunks) - 1)
        bias = cb_ref[...]
        for pc in pieces[lo:hi]:
            bias = bias + run_piece(*pc)


def _inproj(x2, g, w_in, cw, cb, att_dim, seq):
    n, d = x2.shape
    in_cols = w_in.shape[1]
    conv_dim = cw.shape[1]
    row = lambda width: pl.BlockSpec((TM_PROJ, width), lambda i: (i, 0))
    return pl.pallas_call(
        functools.partial(_inproj_kernel, conv_dim=conv_dim, att_dim=att_dim, d_model=d,
                          tiles_per_seq=seq // TM_PROJ),
        grid=(n // TM_PROJ,),
        in_specs=[row(d), _const_spec((1, d)), _const_spec((d, in_cols)),
                  _const_spec(cw.shape), _const_spec(cb.shape)],
        out_specs=[row(conv_dim), row(att_dim), row(att_dim), row(att_dim), row(d), row(d)],
        scratch_shapes=[pltpu.VMEM((HALO + TM_PROJ, conv_dim), F32)],
        out_shape=[
            jax.ShapeDtypeStruct((n, conv_dim), F32),
            jax.ShapeDtypeStruct((n, att_dim), BF16),
            jax.ShapeDtypeStruct((n, att_dim), BF16),
            jax.ShapeDtypeStruct((n, att_dim), BF16),
            jax.ShapeDtypeStruct((n, d), BF16),
            jax.ShapeDtypeStruct((n, d), BF16),
        ],
        compiler_params=pltpu.CompilerParams(
            dimension_semantics=("arbitrary",), vmem_limit_bytes=VMEM_LIMIT_BYTES),
        name="inproj",
    )(x2, g, w_in, cw, cb)


def _softplus2(z):
    return jnp.maximum(jnp.log2(1.0 + jnp.exp2(jnp.minimum(z, SOFTPLUS_CLAMP))), z)


def _attn_kernel(q_ref, k_ref, v_ref, o_ref, acc_ref, carry_ref, qh_ref, vh_ref, live_ref):
    i = pl.program_id(2)
    lane_head = lax.broadcasted_iota(jnp.int32, (1, MXU_DIM), 1) // HEAD_DIM
    head_masks = [lane_head == hh for hh in range(HEADS_PER_GROUP)]

    @pl.when(i == 0)
    def _():
        for hh in range(HEADS_PER_GROUP):
            for r0 in range(0, vh_ref.shape[1], TQ):
                vb = v_ref[r0:r0 + TQ, :]
                vh_ref[hh, r0:r0 + TQ, :] = jnp.where(head_masks[hh], vb, jnp.zeros_like(vb))

    q = q_ref[...]
    for hh in range(HEADS_PER_GROUP):
        qh_ref[hh * TQ:(hh + 1) * TQ, :] = jnp.where(head_masks[hh], q, jnp.zeros_like(q))

    rows = HEADS_PER_GROUP * TQ
    key_idx = lax.broadcasted_iota(jnp.int32, (TQ, TQ), 0)
    col_idx = lax.broadcasted_iota(jnp.int32, (TQ, TQ), 1)
    suffix_ones = (key_idx >= col_idx).astype(BF16)
    t_idx = lax.broadcasted_iota(jnp.int32, (rows, TQ), 0) % TQ
    strictly_causal = lax.broadcasted_iota(jnp.int32, (rows, TQ), 1) < t_idx

    acc_ref[...] = jnp.zeros_like(acc_ref)
    carry_ref[...] = jnp.zeros_like(carry_ref)

    def block(k0, diag):
        kb = k_ref[pl.ds(k0, TQ), :]
        acc = acc_ref[...]
        slowest = None
        for r0 in range(0, rows, SLAB_ROWS):
            sl = slice(r0, r0 + SLAB_ROWS)
            z = lax.dot_general(qh_ref[sl, :], kb, (((1,), (1,)), ((), ())), preferred_element_type=F32)
            sp = _softplus2(z)
            if diag:
                sp = jnp.where(strictly_causal[sl], sp, 0.0)
            suffix = jnp.dot(sp.astype(BF16), suffix_ones, preferred_element_type=F32)
            carry = carry_ref[sl, :]
            a = jnp.exp2(z - suffix + jnp.concatenate([carry] * (TQ // LANES), axis=1))
            if diag:
                a = jnp.where(strictly_causal[sl], a, 0.0)
            a = a.astype(BF16)
            for h0 in range(0, SLAB_ROWS, TQ):
                acc += jnp.dot(a[h0:h0 + TQ, :], vh_ref[(r0 + h0) // TQ, pl.ds(k0, TQ), :],
                               preferred_element_type=F32)
            carry = carry - jnp.broadcast_to(suffix[:, 0:1], carry.shape)
            carry_ref[sl, :] = carry
            slowest = carry if slowest is None else jnp.maximum(slowest, carry)
        acc_ref[...] = acc
        return (jnp.max(slowest) > -EXIT_LOG2).astype(jnp.int32)

    def key_start(back):
        return pl.multiple_of((i - back) * TQ, TQ)

    @pl.when(i == 0)
    def _():
        live_ref[0] = block(key_start(0), True)

    @pl.when(i > 0)
    def _():
        block(key_start(0), True)
        live_ref[0] = block(key_start(1), False)

    def cond(state):
        back, live = state
        return jnp.logical_and(back <= i, live > 0)

    def body(state):
        back, _ = state
        return back + 1, block(key_start(back), False)

    lax.while_loop(cond, body, (jnp.int32(2), live_ref[0]))
    o_ref[...] = acc_ref[...].astype(o_ref.dtype)


def _attention(q, k, v, batch, seq):
    n, att_dim = q.shape
    groups = att_dim // MXU_DIM
    qblocks = seq // TQ
    qspec = pl.BlockSpec((TQ, MXU_DIM), lambda b, g, i: (b * qblocks + i, g))
    kvspec = pl.BlockSpec((seq, MXU_DIM), lambda b, g, i: (b, g))
    return pl.pallas_call(
        _attn_kernel,
        grid=(batch, groups, qblocks),
        in_specs=[qspec, kvspec, kvspec],
        out_specs=qspec,
        out_shape=jax.ShapeDtypeStruct((n, att_dim), BF16),
        scratch_shapes=[pltpu.VMEM((TQ, MXU_DIM), F32),
                        pltpu.VMEM((HEADS_PER_GROUP * TQ, LANES), F32),
                        pltpu.VMEM((HEADS_PER_GROUP * TQ, MXU_DIM), BF16),
                        pltpu.VMEM((HEADS_PER_GROUP, seq, MXU_DIM), BF16),
                        pltpu.SMEM((1,), jnp.int32)],
        compiler_params=pltpu.CompilerParams(
            dimension_semantics=("arbitrary", "arbitrary", "arbitrary"), vmem_limit_bytes=VMEM_LIMIT_BYTES),
        name="stickbreak_attn",
    )(q, k, v)


def _mix_ffn_kernel(y_ref, att_ref, gc_ref, ga_ref, x_ref, lng_ref, lnb_ref, wc_ref, bc_ref, wa_ref, wo_ref,
                    gmix_ref, gpre_ref, wup_ref, wdn_ref, gffn_ref, o_ref, *, d_ff, chunk):
    y = y_ref[...]
    yc = y - jnp.mean(y, axis=-1, keepdims=True)
    y = yc * lax.rsqrt(jnp.mean(yc * yc, axis=-1, keepdims=True) + EPS) * lng_ref[...] + lnb_ref[...]
    y = (y * jax.nn.sigmoid(y)).astype(BF16)
    conv_out = jnp.dot(y, wc_ref[...], preferred_element_type=F32) + bc_ref[...]
    att_out = jnp.dot(att_ref[...], wa_ref[...], preferred_element_type=F32)
    merged = gc_ref[...].astype(F32) * conv_out + ga_ref[...].astype(F32) * att_out
    mix = jnp.dot(merged.astype(BF16), wo_ref[...], preferred_element_type=F32)
    x1 = x_ref[...] + _rms(mix, gmix_ref[...])
    h = _rms(x1, gpre_ref[...]).astype(BF16)
    ff = None
    for c0 in range(0, d_ff, chunk):
        width = min(chunk, d_ff - c0)
        gate = jnp.dot(h, wup_ref[:, c0:c0 + width], preferred_element_type=F32)
        up = jnp.dot(h, wup_ref[:, d_ff + c0:d_ff + c0 + width], preferred_element_type=F32)
        act = (gate * jax.nn.sigmoid(gate) * up).astype(BF16)
        part = jnp.dot(act, wdn_ref[c0:c0 + width, :], preferred_element_type=F32)
        ff = part if ff is None else ff + part
    o_ref[...] = x1 + _rms(ff, gffn_ref[...])


def _mix_ffn(y, att, gc, ga, x2, lng, lnb, wc, bc, wa, wo, gmix, gpre, wup, wdn, gffn):
    n, d = x2.shape
    row = lambda width: pl.BlockSpec((TM_FFN, width), lambda i: (i, 0))
    consts = (lng, lnb, wc, bc, wa, wo, gmix, gpre, wup, wdn, gffn)
    return pl.pallas_call(
        functools.partial(_mix_ffn_kernel, d_ff=wdn.shape[0], chunk=512),
        grid=(n // TM_FFN,),
        in_specs=[row(y.shape[1]), row(att.shape[1]), row(d), row(d), row(d)]
                 + [_const_spec(a.shape) for a in consts],
        out_specs=row(d),
        out_shape=jax.ShapeDtypeStruct((n, d), F32),
        compiler_params=pltpu.CompilerParams(
            dimension_semantics=("arbitrary",), vmem_limit_bytes=VMEM_LIMIT_BYTES),
        name="mix_ffn",
    )(y, att, gc, ga, x2, *consts)


def kernel(x, norm_mix_pre, w_in, conv_dw_w, conv_dw_b, conv_ln_g, conv_ln_b, w_conv_branch, b_conv_branch,
           w_att_branch, w_out, norm_mix_post, norm_ffn_pre, w_ffn_up, w_ffn_down, norm_ffn_post):
    b, s, d = x.shape
    conv_dim = conv_dw_w.shape[1]
    att_dim = w_att_branch.shape[0]
    assert att_dim % MXU_DIM == 0 and s % TQ == 0 and s % TM_PROJ == 0 and (b * s) % TM_FFN == 0
    assert conv_dw_w.shape[0] == CONV_WIDTH and w_in.shape[1] == 2 * conv_dim + 3 * att_dim + 2 * d
    assert HALO >= CONV_WIDTH - 1 - (CONV_WIDTH - 1) % SUBLANES + SUBLANES
    rowvec = lambda a: a.reshape(1, -1).astype(F32)

    x2 = x.reshape(b * s, d)
    y, q, k, v, gc, ga = _inproj(x2, rowvec(norm_mix_pre), w_in.astype(BF16), conv_dw_w.astype(F32),
                                 rowvec(conv_dw_b), att_dim, s)
    att = _attention(q, k, v, b, s)
    out = _mix_ffn(y, att, gc, ga, x2, rowvec(conv_ln_g), rowvec(conv_ln_b), w_conv_branch.astype(BF16),
                   rowvec(b_conv_branch), w_att_branch.astype(BF16), w_out.astype(BF16), rowvec(norm_mix_post),
                   rowvec(norm_ffn_pre), w_ffn_up.astype(BF16), w_ffn_down.astype(BF16), rowvec(norm_ffn_post))
    return out.reshape(b, s, d)
```

```python
import functools

import jax
import jax.numpy as jnp
from jax import lax
from jax.experimental import pallas as pl
from jax.experimental.pallas import tpu as pltpu

F32 = jnp.float32
BF16 = jnp.bfloat16

CONV_WIDTH = 31
HEAD_DIM = 64
EPS = 1e-6

SUBLANES = 8
LANES = 128
MXU_DIM = 256
VMEM_LIMIT_BYTES = 56 * 1024 * 1024

TM_PROJ = 512
TM_FFN = 512
TQ = 256
HEADS_PER_GROUP = MXU_DIM // HEAD_DIM
SLAB_ROWS = 2 * TQ
HALO = 32
CONV_ROWS = 64
PIECE = 512
LOG2E = 1.4426950408889634
EXIT_LOG2 = 160.0
SOFTPLUS_CLAMP = 120.0


def _const_spec(shape):
    return pl.BlockSpec(shape, lambda *_: (0,) * len(shape), pipeline_mode=pl.Buffered(1))


def _rms(x, g):
    return x * lax.rsqrt(jnp.mean(x * x, axis=-1, keepdims=True) + EPS) * g


def _exact_zero_like(x):
    bits = lax.bitcast_convert_type(x, jnp.int32)
    return lax.shift_right_logical(lax.shift_right_logical(bits, 16), 16).astype(F32)


def _conv_chunk(ext_ref, taps, bias, r0):
    rows = CONV_ROWS + SUBLANES
    acc = jnp.broadcast_to(bias, (CONV_ROWS, bias.shape[1]))
    for r in range(SUBLANES):
        part = None
        for d in range(r, CONV_WIDTH, SUBLANES):
            start = r0 + HALO - SUBLANES - (d - r)
            term = taps[CONV_WIDTH - 1 - d] * ext_ref[start:start + rows, :]
            part = term if part is None else part + term
        if r:
            part = pltpu.roll(part, r, axis=0)
        acc = acc + part[SUBLANES:, :]
    return acc


def _inproj_kernel(x_ref, g_ref, w_ref, cw_ref, cb_ref,
                   y_ref, q_ref, k_ref, v_ref, gc_ref, ga_ref, ext_ref,
                   *, conv_dim, att_dim, d_model, tiles_per_seq):
    h = _rms(x_ref[...], g_ref[...]).astype(BF16)

    def proj(c0, width):
        return jnp.dot(h, w_ref[:, c0:c0 + width], preferred_element_type=F32)

    first = (pl.program_id(0) % tiles_per_seq) == 0

    @pl.when(first)
    def _():
        ext_ref[0:HALO, :] = jnp.zeros((HALO, conv_dim), F32)

    @pl.when(jnp.logical_not(first))
    def _():
        ext_ref[0:HALO, :] = ext_ref[TM_PROJ:TM_PROJ + HALO, :]

    ext_ref[HALO:HALO + TM_PROJ, :] = proj(0, conv_dim) * jax.nn.sigmoid(proj(conv_dim, conv_dim))

    c = 2 * conv_dim
    pieces = []
    for ref, scale in ((q_ref, LOG2E * HEAD_DIM ** -0.5), (k_ref, None), (v_ref, None)):
        for o in range(0, att_dim, PIECE):
            pieces.append((ref, o, c + o, scale, False))
        c += att_dim
    for ref in (gc_ref, ga_ref):
        for o in range(0, d_model, PIECE):
            pieces.append((ref, o, c + o, None, True))
        c += d_model

    def run_piece(ref, o, col, scale, gate):
        p = proj(col, PIECE)
        if scale is not None:
            p = p * scale
        if gate:
            p = jax.nn.sigmoid(p)
        ref[:, o:o + PIECE] = p.astype(BF16)
        return _exact_zero_like(p[0:1, 0:conv_dim])

    taps = [cw_ref[j:j + 1, :] for j in range(CONV_WIDTH)]
    bias = cb_ref[...]
    chunks = list(range(0, TM_PROJ, CONV_ROWS))
    for idx, r0 in enumerate(chunks):
        y_ref[r0:r0 + CONV_ROWS, :] = _conv_chunk(ext_ref, taps, bias, r0)
        lo, hi = idx * len(pieces) // (len(chunks) - 1), (idx + 1) * len(pieces) // (len(chunks) - 1)
        bias = cb_ref[...]
        for pc in pieces[lo:hi]:
            bias = bias + run_piece(*pc)


def _inproj(x2, g, w_in, cw, cb, att_dim, seq):
    n, d = x2.shape
    conv_dim = cw.shape[1]
    row = lambda width: pl.BlockSpec((TM_PROJ, width), lambda i: (i, 0))
    return pl.pallas_call(
        functools.partial(_inproj_kernel, conv_dim=conv_dim, att_dim=att_dim, d_model=d,
                          tiles_per_seq=seq // TM_PROJ),
        grid=(n // TM_PROJ,),
        in_specs=[row(d), _const_spec((1, d)), _const_spec(w_in.shape),
                  _const_spec(cw.shape), _const_spec(cb.shape)],
        out_specs=[row(conv_dim), row(att_dim), row(att_dim), row(att_dim), row(d), row(d)],
        scratch_shapes=[pltpu.VMEM((HALO + TM_PROJ, conv_dim), F32)],
        out_shape=[
            jax.ShapeDtypeStruct((n, conv_dim), F32),
            jax.ShapeDtypeStruct((n, att_dim), BF16),
            jax.ShapeDtypeStruct((n, att_dim), BF16),
            jax.ShapeDtypeStruct((n, att_dim), BF16),
            jax.ShapeDtypeStruct((n, d), BF16),
            jax.ShapeDtypeStruct((n, d), BF16),
        ],
        compiler_params=pltpu.CompilerParams(
            dimension_semantics=("arbitrary",), vmem_limit_bytes=VMEM_LIMIT_BYTES),
        name="inproj",
    )(x2, g, w_in, cw, cb)


def _softplus2(z):
    return jnp.maximum(jnp.log2(1.0 + jnp.exp2(jnp.minimum(z, SOFTPLUS_CLAMP))), z)


def _attn_kernel(q_ref, k_ref, v_ref, o_ref, acc_ref, carry_ref, qh_ref, vh_ref, live_ref):
    i = pl.program_id(1)
    n_heads = q_ref.shape[1] // HEAD_DIM
    lane_head = lax.broadcasted_iota(jnp.int32, (1, MXU_DIM), 1) // HEAD_DIM
    head_masks = [lane_head == hh for hh in range(HEADS_PER_GROUP)]
    group_lanes = lambda head: slice(head // HEADS_PER_GROUP * MXU_DIM, (head // HEADS_PER_GROUP + 1) * MXU_DIM)

    @pl.when(i == 0)
    def _():
        for head in range(n_heads):
            for r0 in range(0, vh_ref.shape[1], TQ):
                vb = v_ref[r0:r0 + TQ, group_lanes(head)]
                vh_ref[head, r0:r0 + TQ, :] = jnp.where(head_masks[head % HEADS_PER_GROUP], vb, jnp.zeros_like(vb))

    for head in range(n_heads):
        q = q_ref[:, group_lanes(head)]
        qh_ref[head * TQ:(head + 1) * TQ, :] = jnp.where(head_masks[head % HEADS_PER_GROUP], q, jnp.zeros_like(q))

    key_idx = lax.broadcasted_iota(jnp.int32, (TQ, TQ), 0)
    col_idx = lax.broadcasted_iota(jnp.int32, (TQ, TQ), 1)
    suffix_ones = (key_idx >= col_idx).astype(BF16)
    t_idx = lax.broadcasted_iota(jnp.int32, (SLAB_ROWS, TQ), 0) % TQ
    strictly_causal = lax.broadcasted_iota(jnp.int32, (SLAB_ROWS, TQ), 1) < t_idx

    acc_ref[...] = jnp.zeros_like(acc_ref)
    carry_ref[...] = jnp.zeros_like(carry_ref)

    def block(k0, diag):
        slowest = None
        for r0 in range(0, n_heads * TQ, SLAB_ROWS):
            sl = slice(r0, r0 + SLAB_ROWS)
            lanes = group_lanes(r0 // TQ)
            z = lax.dot_general(qh_ref[sl, :], k_ref[pl.ds(k0, TQ), lanes], (((1,), (1,)), ((), ())),
                                preferred_element_type=F32)
            sp = _softplus2(z)
            if diag:
                sp = jnp.where(strictly_causal, sp, 0.0)
            suffix = jnp.dot(sp.astype(BF16), suffix_ones, preferred_element_type=F32)
            carry = carry_ref[sl, :]
            a = jnp.exp2(z - suffix + jnp.concatenate([carry] * (TQ // LANES), axis=1))
            if diag:
                a = jnp.where(strictly_causal, a, 0.0)
            a = a.astype(BF16)
            acc = acc_ref[:, lanes]
            for h0 in range(0, SLAB_ROWS, TQ):
                acc += jnp.dot(a[h0:h0 + TQ, :], vh_ref[(r0 + h0) // TQ, pl.ds(k0, TQ), :],
                               preferred_element_type=F32)
            acc_ref[:, lanes] = acc
            carry = carry - jnp.broadcast_to(suffix[:, 0:1], carry.shape)
            carry_ref[sl, :] = carry
            slowest = carry if slowest is None else jnp.maximum(slowest, carry)
        return (jnp.max(slowest) > -EXIT_LOG2).astype(jnp.int32)

    def key_start(back):
        return pl.multiple_of((i - back) * TQ, TQ)

    @pl.when(i == 0)
    def _():
        live_ref[0] = block(key_start(0), True)

    @pl.when(i > 0)
    def _():
        block(key_start(0), True)
        live_ref[0] = block(key_start(1), False)

    def cond(state):
        back, live = state
        return jnp.logical_and(back <= i, live > 0)

    def body(state):
        back, _ = state
        return back + 1, block(key_start(back), False)

    lax.while_loop(cond, body, (jnp.int32(2), live_ref[0]))
    o_ref[...] = acc_ref[...].astype(o_ref.dtype)


def _attention(q, k, v, batch, seq):
    n, att_dim = q.shape
    n_heads = att_dim // HEAD_DIM
    qblocks = seq // TQ
    qspec = pl.BlockSpec((TQ, att_dim), lambda b, i: (b * qblocks + i, 0))
    kvspec = pl.BlockSpec((seq, att_dim), lambda b, i: (b, 0))
    return pl.pallas_call(
        _attn_kernel,
        grid=(batch, qblocks),
        in_specs=[qspec, kvspec, kvspec],
        out_specs=qspec,
        out_shape=jax.ShapeDtypeStruct((n, att_dim), BF16),
        scratch_shapes=[pltpu.VMEM((TQ, att_dim), F32),
                        pltpu.VMEM((n_heads * TQ, LANES), F32),
                        pltpu.VMEM((n_heads * TQ, MXU_DIM), BF16),
                        pltpu.VMEM((n_heads, seq, MXU_DIM), BF16),
                        pltpu.SMEM((1,), jnp.int32)],
        compiler_params=pltpu.CompilerParams(
            dimension_semantics=("arbitrary", "arbitrary"), vmem_limit_bytes=VMEM_LIMIT_BYTES),
        name="stickbreak_attn",
    )(q, k, v)


def _mix_ffn_kernel(y_ref, att_ref, gc_ref, ga_ref, x_ref, lng_ref, lnb_ref, wc_ref, bc_ref, wa_ref, wo_ref,
                    gmix_ref, gpre_ref, wup_ref, wdn_ref, gffn_ref, o_ref, *, d_ff, chunk):
    y = y_ref[...]
    yc = y - jnp.mean(y, axis=-1, keepdims=True)
    y = yc * lax.rsqrt(jnp.mean(yc * yc, axis=-1, keepdims=True) + EPS) * lng_ref[...] + lnb_ref[...]
    y = (y * jax.nn.sigmoid(y)).astype(BF16)
    conv_out = jnp.dot(y, wc_ref[...], preferred_element_type=F32) + bc_ref[...]
    att_out = jnp.dot(att_ref[...], wa_ref[...], preferred_element_type=F32)
    merged = gc_ref[...].astype(F32) * conv_out + ga_ref[...].astype(F32) * att_out
    mix = jnp.dot(merged.astype(BF16), wo_ref[...], preferred_element_type=F32)
    x1 = x_ref[...] + _rms(mix, gmix_ref[...])
    h = _rms(x1, gpre_ref[...]).astype(BF16)
    ff = None
    for c0 in range(0, d_ff, chunk):
        width = min(chunk, d_ff - c0)
        gate = jnp.dot(h, wup_ref[:, c0:c0 + width], preferred_element_type=F32)
        up = jnp.dot(h, wup_ref[:, d_ff + c0:d_ff + c0 + width], preferred_element_type=F32)
        act = (gate * jax.nn.sigmoid(gate) * up).astype(BF16)
        part = jnp.dot(act, wdn_ref[c0:c0 + width, :], preferred_element_type=F32)
        ff = part if ff is None else ff + part
    o_ref[...] = x1 + _rms(ff, gffn_ref[...])


def _mix_ffn(y, att, gc, ga, x2, lng, lnb, wc, bc, wa, wo, gmix, gpre, wup, wdn, gffn):
    n, d = x2.shape
    row = lambda width: pl.BlockSpec((TM_FFN, width), lambda i: (i, 0))
    consts = (lng, lnb, wc, bc, wa, wo, gmix, gpre, wup, wdn, gffn)
    return pl.pallas_call(
        functools.partial(_mix_ffn_kernel, d_ff=wdn.shape[0], chunk=512),
        grid=(n // TM_FFN,),
        in_specs=[row(y.shape[1]), row(att.shape[1]), row(d), row(d), row(d)]
                 + [_const_spec(a.shape) for a in consts],
        out_specs=row(d),
        out_shape=jax.ShapeDtypeStruct((n, d), F32),
        compiler_params=pltpu.CompilerParams(
            dimension_semantics=("arbitrary",), vmem_limit_bytes=VMEM_LIMIT_BYTES),
        name="mix_ffn",
    )(y, att, gc, ga, x2, *consts)


def kernel(x, norm_mix_pre, w_in, conv_dw_w, conv_dw_b, conv_ln_g, conv_ln_b, w_conv_branch, b_conv_branch,
           w_att_branch, w_out, norm_mix_post, norm_ffn_pre, w_ffn_up, w_ffn_down, norm_ffn_post):
    b, s, d = x.shape
    conv_dim = conv_dw_w.shape[1]
    att_dim = w_att_branch.shape[0]
    assert att_dim % MXU_DIM == 0 and s % TQ == 0 and s % TM_PROJ == 0 and (b * s) % TM_FFN == 0
    assert conv_dw_w.shape[0] == CONV_WIDTH and w_in.shape[1] == 2 * conv_dim + 3 * att_dim + 2 * d
    assert HALO >= CONV_WIDTH - 1 - (CONV_WIDTH - 1) % SUBLANES + SUBLANES
    rowvec = lambda a: a.reshape(1, -1).astype(F32)

    x2 = x.reshape(b * s, d)
    y, q, k, v, gc, ga = _inproj(x2, rowvec(norm_mix_pre), w_in.astype(BF16), conv_dw_w.astype(F32),
                                 rowvec(conv_dw_b), att_dim, s)
    att = _attention(q, k, v, b, s)
    out = _mix_ffn(y, att, gc, ga, x2, rowvec(conv_ln_g), rowvec(conv_ln_b), w_conv_branch.astype(BF16),
                   rowvec(b_conv_branch), w_att_branch.astype(BF16), w_out.astype(BF16), rowvec(norm_mix_post),
                   rowvec(norm_ffn_pre), w_ffn_up.astype(BF16), w_ffn_down.astype(BF16), rowvec(norm_ffn_post))
    return out.reshape(b, s, d)
```

```python
import functools

import jax
import jax.numpy as jnp
from jax import lax
from jax.experimental import pallas as pl
from jax.experimental.pallas import tpu as pltpu

F32 = jnp.float32
BF16 = jnp.bfloat16

CONV_WIDTH = 31
HEAD_DIM = 64
EPS = 1e-6

SUBLANES = 8
LANES = 128
MXU_DIM = 256
VMEM_LIMIT_BYTES = 56 * 1024 * 1024

TM_PROJ = 512
TM_FFN = 512
TQ = 256
HEADS_PER_GROUP = MXU_DIM // HEAD_DIM
SLAB_ROWS = 2 * TQ
HALO = 32
CONV_ROWS = 64
PIECE = 512
LOG2E = 1.4426950408889634
EXIT_LOG2 = 160.0
SOFTPLUS_CLAMP = 120.0


def _const_spec(shape):
    return pl.BlockSpec(shape, lambda *_: (0,) * len(shape), pipeline_mode=pl.Buffered(1))


def _rms(x, g):
    return x * lax.rsqrt(jnp.mean(x * x, axis=-1, keepdims=True) + EPS) * g


def _exact_zero_like(x):
    bits = lax.bitcast_convert_type(x, jnp.int32)
    return lax.shift_right_logical(lax.shift_right_logical(bits, 16), 16).astype(F32)


def _conv_chunk(ext_ref, taps, bias, r0):
    rows = CONV_ROWS + SUBLANES
    acc = jnp.broadcast_to(bias, (CONV_ROWS, bias.shape[1]))
    for r in range(SUBLANES):
        part = None
        for d in range(r, CONV_WIDTH, SUBLANES):
            start = r0 + HALO - SUBLANES - (d - r)
            term = taps[CONV_WIDTH - 1 - d] * ext_ref[start:start + rows, :]
            part = term if part is None else part + term
        if r:
            part = pltpu.roll(part, r, axis=0)
        acc = acc + part[SUBLANES:, :]
    return acc


def _cast_once(src_ref, dst_ref):
    @pl.when(pl.program_id(0) == 0)
    def _():
        for c0 in range(0, src_ref.shape[1], PIECE):
            dst_ref[:, c0:c0 + PIECE] = src_ref[:, c0:c0 + PIECE].astype(BF16)


def _inproj_kernel(x_ref, g_ref, w32_ref, cw_ref, cb_ref,
                   y_ref, q_ref, k_ref, v_ref, gc_ref, ga_ref, ext_ref, w_ref,
                   *, conv_dim, att_dim, d_model, tiles_per_seq):
    _cast_once(w32_ref, w_ref)
    h = _rms(x_ref[...], g_ref[...]).astype(BF16)

    def proj(c0, width):
        return jnp.dot(h, w_ref[:, c0:c0 + width], preferred_element_type=F32)

    first = (pl.program_id(0) % tiles_per_seq) == 0

    @pl.when(first)
    def _():
        ext_ref[0:HALO, :] = jnp.zeros((HALO, conv_dim), F32)

    @pl.when(jnp.logical_not(first))
    def _():
        ext_ref[0:HALO, :] = ext_ref[TM_PROJ:TM_PROJ + HALO, :]

    ext_ref[HALO:HALO + TM_PROJ, :] = proj(0, conv_dim) * jax.nn.sigmoid(proj(conv_dim, conv_dim))

    c = 2 * conv_dim
    pieces = []
    for ref, scale in ((q_ref, LOG2E * HEAD_DIM ** -0.5), (k_ref, None), (v_ref, None)):
        for o in range(0, att_dim, PIECE):
            pieces.append((ref, o, c + o, scale, False))
        c += att_dim
    for ref in (gc_ref, ga_ref):
        for o in range(0, d_model, PIECE):
            pieces.append((ref, o, c + o, None, True))
        c += d_model

    def run_piece(ref, o, col, scale, gate):
        p = proj(col, PIECE)
        if scale is not None:
            p = p * scale
        if gate:
            p = jax.nn.sigmoid(p)
        ref[:, o:o + PIECE] = p.astype(BF16)
        return _exact_zero_like(p[0:1, 0:conv_dim])

    taps = [cw_ref[j:j + 1, :] for j in range(CONV_WIDTH)]
    bias = cb_ref[...]
    chunks = list(range(0, TM_PROJ, CONV_ROWS))
    for idx, r0 in enumerate(chunks):
        y_ref[r0:r0 + CONV_ROWS, :] = _conv_chunk(ext_ref, taps, bias, r0)
        lo, hi = idx * len(pieces) // (len(chunks) - 1), (idx + 1) * len(pieces) // (len(chunks) - 1)
        bias = cb_ref[...]
        for pc in pieces[lo:hi]:
            bias = bias + run_piece(*pc)


def _inproj(x2, g, w_in, cw, cb, att_dim, seq):
    n, d = x2.shape
    conv_dim = cw.shape[1]
    row = lambda width: pl.BlockSpec((TM_PROJ, width), lambda i: (i, 0))
    return pl.pallas_call(
        functools.partial(_inproj_kernel, conv_dim=conv_dim, att_dim=att_dim, d_model=d,
                          tiles_per_seq=seq // TM_PROJ),
        grid=(n // TM_PROJ,),
        in_specs=[row(d), _const_spec((1, d)), _const_spec(w_in.shape),
                  _const_spec(cw.shape), _const_spec(cb.shape)],
        out_specs=[row(conv_dim), row(att_dim), row(att_dim), row(att_dim), row(d), row(d)],
        scratch_shapes=[pltpu.VMEM((HALO + TM_PROJ, conv_dim), F32), pltpu.VMEM(w_in.shape, BF16)],
        out_shape=[
            jax.ShapeDtypeStruct((n, conv_dim), F32),
            jax.ShapeDtypeStruct((n, att_dim), BF16),
            jax.ShapeDtypeStruct((n, att_dim), BF16),
            jax.ShapeDtypeStruct((n, att_dim), BF16),
            jax.ShapeDtypeStruct((n, d), BF16),
            jax.ShapeDtypeStruct((n, d), BF16),
        ],
        compiler_params=pltpu.CompilerParams(
            dimension_semantics=("arbitrary",), vmem_limit_bytes=VMEM_LIMIT_BYTES),
        name="inproj",
    )(x2, g, w_in, cw, cb)


def _softplus2(z):
    return jnp.maximum(jnp.log2(1.0 + jnp.exp2(jnp.minimum(z, SOFTPLUS_CLAMP))), z)


def _attn_kernel(q_ref, k_ref, v_ref, o_ref, acc_ref, carry_ref, qh_ref, vh_ref, live_ref):
    i = pl.program_id(1)
    n_heads = q_ref.shape[1] // HEAD_DIM
    lane_head = lax.broadcasted_iota(jnp.int32, (1, MXU_DIM), 1) // HEAD_DIM
    head_masks = [lane_head == hh for hh in range(HEADS_PER_GROUP)]
    group_lanes = lambda head: slice(head // HEADS_PER_GROUP * MXU_DIM, (head // HEADS_PER_GROUP + 1) * MXU_DIM)

    @pl.when(i == 0)
    def _():
        for head in range(n_heads):
            for r0 in range(0, vh_ref.shape[1], TQ):
                vb = v_ref[r0:r0 + TQ, group_lanes(head)]
                vh_ref[head, r0:r0 + TQ, :] = jnp.where(head_masks[head % HEADS_PER_GROUP], vb, jnp.zeros_like(vb))

    for head in range(n_heads):
        q = q_ref[:, group_lanes(head)]
        qh_ref[head * TQ:(head + 1) * TQ, :] = jnp.where(head_masks[head % HEADS_PER_GROUP], q, jnp.zeros_like(q))

    key_idx = lax.broadcasted_iota(jnp.int32, (TQ, TQ), 0)
    col_idx = lax.broadcasted_iota(jnp.int32, (TQ, TQ), 1)
    suffix_ones = (key_idx >= col_idx).astype(BF16)
    t_idx = lax.broadcasted_iota(jnp.int32, (SLAB_ROWS, TQ), 0) % TQ
    strictly_causal = lax.broadcasted_iota(jnp.int32, (SLAB_ROWS, TQ), 1) < t_idx

    acc_ref[...] = jnp.zeros_like(acc_ref)
    carry_ref[...] = jnp.zeros_like(carry_ref)

    def block(k0, diag):
        slowest = None
        for r0 in range(0, n_heads * TQ, SLAB_ROWS):
            sl = slice(r0, r0 + SLAB_ROWS)
            lanes = group_lanes(r0 // TQ)
            z = lax.dot_general(qh_ref[sl, :], k_ref[pl.ds(k0, TQ), lanes], (((1,), (1,)), ((), ())),
                                preferred_element_type=F32)
            sp = _softplus2(z)
            if diag:
                sp = jnp.where(strictly_causal, sp, 0.0)
            suffix = jnp.dot(sp.astype(BF16), suffix_ones, preferred_element_type=F32)
            carry = carry_ref[sl, :]
            a = jnp.exp2(z - suffix + jnp.concatenate([carry] * (TQ // LANES), axis=1))
            if diag:
                a = jnp.where(strictly_causal, a, 0.0)
            a = a.astype(BF16)
            acc = acc_ref[:, lanes]
            for h0 in range(0, SLAB_ROWS, TQ):
                acc += jnp.dot(a[h0:h0 + TQ, :], vh_ref[(r0 + h0) // TQ, pl.ds(k0, TQ), :],
                               preferred_element_type=F32)
            acc_ref[:, lanes] = acc
            carry = carry - jnp.broadcast_to(suffix[:, 0:1], carry.shape)
            carry_ref[sl, :] = carry
            slowest = carry if slowest is None else jnp.maximum(slowest, carry)
        return (jnp.max(slowest) > -EXIT_LOG2).astype(jnp.int32)

    def key_start(back):
        return pl.multiple_of((i - back) * TQ, TQ)

    @pl.when(i == 0)
    def _():
        live_ref[0] = block(key_start(0), True)

    @pl.when(i > 0)
    def _():
        block(key_start(0), True)
        live_ref[0] = block(key_start(1), False)

    def cond(state):
        back, live = state
        return jnp.logical_and(back <= i, live > 0)

    def body(state):
        back, _ = state
        return back + 1, block(key_start(back), False)

    lax.while_loop(cond, body, (jnp.int32(2), live_ref[0]))
    o_ref[...] = acc_ref[...].astype(o_ref.dtype)


def _attention(q, k, v, batch, seq):
    n, att_dim = q.shape
    n_heads = att_dim // HEAD_DIM
    qblocks = seq // TQ
    qspec = pl.BlockSpec((TQ, att_dim), lambda b, i: (b * qblocks + i, 0))
    kvspec = pl.BlockSpec((seq, att_dim), lambda b, i: (b, 0))
    return pl.pallas_call(
        _attn_kernel,
        grid=(batch, qblocks),
        in_specs=[qspec, kvspec, kvspec],
        out_specs=qspec,
        out_shape=jax.ShapeDtypeStruct((n, att_dim), BF16),
        scratch_shapes=[pltpu.VMEM((TQ, att_dim), F32),
                        pltpu.VMEM((n_heads * TQ, LANES), F32),
                        pltpu.VMEM((n_heads * TQ, MXU_DIM), BF16),
                        pltpu.VMEM((n_heads, seq, MXU_DIM), BF16),
                        pltpu.SMEM((1,), jnp.int32)],
        compiler_params=pltpu.CompilerParams(
            dimension_semantics=("arbitrary", "arbitrary"), vmem_limit_bytes=VMEM_LIMIT_BYTES),
        name="stickbreak_attn",
    )(q, k, v)


def _mix_ffn_kernel(y_ref, att_ref, gc_ref, ga_ref, x_ref, lng_ref, lnb_ref, wc32_ref, bc_ref, wa32_ref,
                    wo32_ref, gmix_ref, gpre_ref, wup_ref, wdn_ref, gffn_ref, o_ref, wc_ref, wa_ref, wo_ref,
                    *, d_ff, chunk):
    for src_ref, dst_ref in ((wc32_ref, wc_ref), (wa32_ref, wa_ref), (wo32_ref, wo_ref)):
        _cast_once(src_ref, dst_ref)
    y = y_ref[...]
    yc = y - jnp.mean(y, axis=-1, keepdims=True)
    y = yc * lax.rsqrt(jnp.mean(yc * yc, axis=-1, keepdims=True) + EPS) * lng_ref[...] + lnb_ref[...]
    y = (y * jax.nn.sigmoid(y)).astype(BF16)
    conv_out = jnp.dot(y, wc_ref[...], preferred_element_type=F32) + bc_ref[...]
    att_out = jnp.dot(att_ref[...], wa_ref[...], preferred_element_type=F32)
    merged = gc_ref[...].astype(F32) * conv_out + ga_ref[...].astype(F32) * att_out
    mix = jnp.dot(merged.astype(BF16), wo_ref[...], preferred_element_type=F32)
    x1 = x_ref[...] + _rms(mix, gmix_ref[...])
    h = _rms(x1, gpre_ref[...]).astype(BF16)
    ff = None
    for c0 in range(0, d_ff, chunk):
        width = min(chunk, d_ff - c0)
        gate = jnp.dot(h, wup_ref[:, c0:c0 + width], preferred_element_type=F32)
        up = jnp.dot(h, wup_ref[:, d_ff + c0:d_ff + c0 + width], preferred_element_type=F32)
        act = (gate * jax.nn.sigmoid(gate) * up).astype(BF16)
        part = jnp.dot(act, wdn_ref[c0:c0 + width, :], preferred_element_type=F32)
        ff = part if ff is None else ff + part
    o_ref[...] = x1 + _rms(ff, gffn_ref[...])


def _mix_ffn(y, att, gc, ga, x2, lng, lnb, wc, bc, wa, wo, gmix, gpre, wup, wdn, gffn):
    n, d = x2.shape
    row = lambda width: pl.BlockSpec((TM_FFN, width), lambda i: (i, 0))
    consts = (lng, lnb, wc, bc, wa, wo, gmix, gpre, wup, wdn, gffn)
    return pl.pallas_call(
        functools.partial(_mix_ffn_kernel, d_ff=wdn.shape[0], chunk=512),
        grid=(n // TM_FFN,),
        in_specs=[row(y.shape[1]), row(att.shape[1]), row(d), row(d), row(d)]
                 + [_const_spec(a.shape) for a in consts],
        out_specs=row(d),
        out_shape=jax.ShapeDtypeStruct((n, d), F32),
        scratch_shapes=[pltpu.VMEM(a.shape, BF16) for a in (wc, wa, wo)],
        compiler_params=pltpu.CompilerParams(
            dimension_semantics=("arbitrary",), vmem_limit_bytes=VMEM_LIMIT_BYTES),
        name="mix_ffn",
    )(y, att, gc, ga, x2, *consts)


def kernel(x, norm_mix_pre, w_in, conv_dw_w, conv_dw_b, conv_ln_g, conv_ln_b, w_conv_branch, b_conv_branch,
           w_att_branch, w_out, norm_mix_post, norm_ffn_pre, w_ffn_up, w_ffn_down, norm_ffn_post):
    b, s, d = x.shape
    conv_dim = conv_dw_w.shape[1]
    att_dim = w_att_branch.shape[0]
    assert att_dim % MXU_DIM == 0 and s % TQ == 0 and s % TM_PROJ == 0 and (b * s) % TM_FFN == 0
    assert conv_dw_w.shape[0] == CONV_WIDTH and w_in.shape[1] == 2 * conv_dim + 3 * att_dim + 2 * d
    assert HALO >= CONV_WIDTH - 1 - (CONV_WIDTH - 1) % SUBLANES + SUBLANES
    rowvec = lambda a: a.reshape(1, -1).astype(F32)

    x2 = x.reshape(b * s, d)
    y, q, k, v, gc, ga = _inproj(x2, rowvec(norm_mix_pre), w_in.astype(F32), conv_dw_w.astype(F32),
                                 rowvec(conv_dw_b), att_dim, s)
    att = _attention(q, k, v, b, s)
    out = _mix_ffn(y, att, gc, ga, x2, rowvec(conv_ln_g), rowvec(conv_ln_b), w_conv_branch.astype(F32),
                   rowvec(b_conv_branch), w_att_branch.astype(F32), w_out.astype(F32), rowvec(norm_mix_post),
                   rowvec(norm_ffn_pre), w_ffn_up.astype(BF16), w_ffn_down.astype(BF16), rowvec(norm_ffn_post))
    return out.reshape(b, s, d)
```

```python
import functools

import jax
import jax.numpy as jnp
from jax import lax
from jax.experimental import pallas as pl
from jax.experimental.pallas import tpu as pltpu

F32 = jnp.float32
BF16 = jnp.bfloat16

CONV_WIDTH = 31
HEAD_DIM = 64
EPS = 1e-6

SUBLANES = 8
LANES = 128
MXU_DIM = 256
VMEM_LIMIT_BYTES = 56 * 1024 * 1024

TM_PROJ = 512
TM_FFN = 512
TQ = 256
HEADS_PER_GROUP = MXU_DIM // HEAD_DIM
SLAB_ROWS = 2 * TQ
HALO = 32
CONV_ROWS = 32
PIECE = 256
FFN_CHUNK = 512
LOG2E = 1.4426950408889634
EXIT_LOG2 = 160.0
SOFTPLUS_CLAMP = 120.0


def _const_spec(shape):
    return pl.BlockSpec(shape, lambda *_: (0,) * len(shape), pipeline_mode=pl.Buffered(1))


def _rms(x, g):
    return x * lax.rsqrt(jnp.mean(x * x, axis=-1, keepdims=True) + EPS) * g


def _exact_zero_like(x):
    bits = lax.bitcast_convert_type(x, jnp.int32)
    return lax.shift_right_logical(lax.shift_right_logical(bits, 16), 16).astype(F32)


def _conv_chunk(ext_ref, taps, bias, r0):
    rows = CONV_ROWS + SUBLANES
    acc = jnp.broadcast_to(bias, (CONV_ROWS, bias.shape[1]))
    for r in range(SUBLANES):
        part = None
        for d in range(r, CONV_WIDTH, SUBLANES):
            start = r0 + HALO - SUBLANES - (d - r)
            term = taps[CONV_WIDTH - 1 - d] * ext_ref[start:start + rows, :]
            part = term if part is None else part + term
        if r:
            part = pltpu.roll(part, r, axis=0)
        acc = acc + part[SUBLANES:, :]
    return acc


def _cast_once(src_ref, dst_ref):
    @pl.when(pl.program_id(0) == 0)
    def _():
        for c0 in range(0, src_ref.shape[1], PIECE):
            dst_ref[:, c0:c0 + PIECE] = src_ref[:, c0:c0 + PIECE].astype(BF16)


def _inproj_kernel(x_ref, g_ref, w32_ref, cw_ref, cb_ref,
                   y_ref, q_ref, k_ref, v_ref, gc_ref, ga_ref, ext_ref, w_ref,
                   *, conv_dim, att_dim, d_model, tiles_per_seq):
    _cast_once(w32_ref, w_ref)
    h = _rms(x_ref[...], g_ref[...]).astype(BF16)

    def proj(c0, width):
        return jnp.dot(h, w_ref[:, c0:c0 + width], preferred_element_type=F32)

    first = (pl.program_id(0) % tiles_per_seq) == 0

    @pl.when(first)
    def _():
        ext_ref[0:HALO, :] = jnp.zeros((HALO, conv_dim), F32)

    @pl.when(jnp.logical_not(first))
    def _():
        ext_ref[0:HALO, :] = ext_ref[TM_PROJ:TM_PROJ + HALO, :]

    ext_ref[HALO:HALO + TM_PROJ, :] = proj(0, conv_dim) * jax.nn.sigmoid(proj(conv_dim, conv_dim))

    c = 2 * conv_dim
    pieces = []
    for ref, scale in ((q_ref, LOG2E * HEAD_DIM ** -0.5), (k_ref, None), (v_ref, None)):
        for o in range(0, att_dim, PIECE):
            pieces.append((ref, o, c + o, scale, False))
        c += att_dim
    for ref in (gc_ref, ga_ref):
        for o in range(0, d_model, PIECE):
            pieces.append((ref, o, c + o, None, True))
        c += d_model

    def run_piece(ref, o, col, scale, gate):
        p = proj(col, PIECE)
        if scale is not None:
            p = p * scale
        if gate:
            p = jax.nn.sigmoid(p)
        ref[:, o:o + PIECE] = p.astype(BF16)
        return _exact_zero_like(p[0:1, 0:1])

    taps = [cw_ref[j:j + 1, :] for j in range(CONV_WIDTH)]
    bias = cb_ref[...]
    chunks = list(range(0, TM_PROJ, CONV_ROWS))
    for idx, r0 in enumerate(chunks):
        y_ref[r0:r0 + CONV_ROWS, :] = _conv_chunk(ext_ref, taps, bias, r0)
        lo, hi = idx * len(pieces) // (len(chunks) - 1), (idx + 1) * len(pieces) // (len(chunks) - 1)
        bias = cb_ref[...]
        for pc in pieces[lo:hi]:
            bias = bias + run_piece(*pc)


def _inproj(x2, g, w_in, cw, cb, att_dim, seq):
    n, d = x2.shape
    conv_dim = cw.shape[1]
    row = lambda width: pl.BlockSpec((TM_PROJ, width), lambda i: (i, 0))
    return pl.pallas_call(
        functools.partial(_inproj_kernel, conv_dim=conv_dim, att_dim=att_dim, d_model=d,
                          tiles_per_seq=seq // TM_PROJ),
        grid=(n // TM_PROJ,),
        in_specs=[row(d), _const_spec((1, d)), _const_spec(w_in.shape),
                  _const_spec(cw.shape), _const_spec(cb.shape)],
        out_specs=[row(conv_dim), row(att_dim), row(att_dim), row(att_dim), row(d), row(d)],
        scratch_shapes=[pltpu.VMEM((HALO + TM_PROJ, conv_dim), F32), pltpu.VMEM(w_in.shape, BF16)],
        out_shape=[
            jax.ShapeDtypeStruct((n, conv_dim), F32),
            jax.ShapeDtypeStruct((n, att_dim), BF16),
            jax.ShapeDtypeStruct((n, att_dim), BF16),
            jax.ShapeDtypeStruct((n, att_dim), BF16),
            jax.ShapeDtypeStruct((n, d), BF16),
            jax.ShapeDtypeStruct((n, d), BF16),
        ],
        compiler_params=pltpu.CompilerParams(
            dimension_semantics=("arbitrary",), vmem_limit_bytes=VMEM_LIMIT_BYTES),
        name="inproj",
    )(x2, g, w_in, cw, cb)


def _softplus2(z):
    return jnp.maximum(jnp.log2(1.0 + jnp.exp2(jnp.minimum(z, SOFTPLUS_CLAMP))), z)


def _attn_kernel(q_ref, k_ref, v_ref, o_ref, acc_ref, carry_ref, qh_ref, vh_ref, live_ref):
    i = pl.program_id(1)
    n_heads = q_ref.shape[1] // HEAD_DIM
    lane_head = lax.broadcasted_iota(jnp.int32, (1, MXU_DIM), 1) // HEAD_DIM
    head_masks = [lane_head == hh for hh in range(HEADS_PER_GROUP)]
    group_lanes = lambda head: slice(head // HEADS_PER_GROUP * MXU_DIM, (head // HEADS_PER_GROUP + 1) * MXU_DIM)

    @pl.when(i == 0)
    def _():
        for head in range(n_heads):
            for r0 in range(0, vh_ref.shape[1], TQ):
                vb = v_ref[r0:r0 + TQ, group_lanes(head)]
                vh_ref[head, r0:r0 + TQ, :] = jnp.where(head_masks[head % HEADS_PER_GROUP], vb, jnp.zeros_like(vb))

    for head in range(n_heads):
        q = q_ref[:, group_lanes(head)]
        qh_ref[head * TQ:(head + 1) * TQ, :] = jnp.where(head_masks[head % HEADS_PER_GROUP], q, jnp.zeros_like(q))

    key_idx = lax.broadcasted_iota(jnp.int32, (TQ, TQ), 0)
    col_idx = lax.broadcasted_iota(jnp.int32, (TQ, TQ), 1)
    suffix_ones = (key_idx >= col_idx).astype(BF16)
    half = TQ // 2
    below_diag = (lax.broadcasted_iota(jnp.int32, (half, half), 1)
                  < lax.broadcasted_iota(jnp.int32, (half, half), 0))

    def causal_quadrants(fn):
        lo, hi = slice(0, half), slice(half, TQ)
        out = []
        for h0 in range(0, SLAB_ROWS, TQ):
            top, bottom = slice(h0, h0 + half), slice(h0 + half, h0 + TQ)
            upper_left = jnp.where(below_diag, fn(top, lo), 0.0)
            lower_right = jnp.where(below_diag, fn(bottom, hi), 0.0)
            out.append(jnp.concatenate([upper_left, jnp.zeros_like(upper_left)], axis=1))
            out.append(jnp.concatenate([fn(bottom, lo), lower_right], axis=1))
        return jnp.concatenate(out, axis=0)

    acc_ref[...] = jnp.zeros_like(acc_ref)
    carry_ref[...] = jnp.zeros_like(carry_ref)

    def block(k0, diag):
        slowest = None
        for r0 in range(0, n_heads * TQ, SLAB_ROWS):
            sl = slice(r0, r0 + SLAB_ROWS)
            lanes = group_lanes(r0 // TQ)
            z = lax.dot_general(qh_ref[sl, :], k_ref[pl.ds(k0, TQ), lanes], (((1,), (1,)), ((), ())),
                                preferred_element_type=F32)
            sp = causal_quadrants(lambda r, c: _softplus2(z[r, c])) if diag else _softplus2(z)
            suffix = jnp.dot(sp.astype(BF16), suffix_ones, preferred_element_type=F32)
            carry = carry_ref[sl, :]
            if diag:
                a = causal_quadrants(lambda r, c: jnp.exp2(z[r, c] - suffix[r, c] + carry[r, :]))
            else:
                a = jnp.exp2(z - suffix + jnp.concatenate([carry] * (TQ // LANES), axis=1))
            a = a.astype(BF16)
            acc = acc_ref[:, lanes]
            for h0 in range(0, SLAB_ROWS, TQ):
                acc += jnp.dot(a[h0:h0 + TQ, :], vh_ref[(r0 + h0) // TQ, pl.ds(k0, TQ), :],
                               preferred_element_type=F32)
            acc_ref[:, lanes] = acc
            carry = carry - jnp.broadcast_to(suffix[:, 0:1], carry.shape)
            carry_ref[sl, :] = carry
            slowest = carry if slowest is None else jnp.maximum(slowest, carry)
        return (jnp.max(slowest) > -EXIT_LOG2).astype(jnp.int32)

    def key_start(back):
        return pl.multiple_of((i - back) * TQ, TQ)

    @pl.when(i == 0)
    def _():
        live_ref[0] = block(key_start(0), True)

    @pl.when(i > 0)
    def _():
        block(key_start(0), True)
        live_ref[0] = block(key_start(1), False)

    def cond(state):
        back, live = state
        return jnp.logical_and(back <= i, live > 0)

    def body(state):
        back, _ = state
        return back + 1, block(key_start(back), False)

    lax.while_loop(cond, body, (jnp.int32(2), live_ref[0]))
    o_ref[...] = acc_ref[...].astype(o_ref.dtype)


def _attention(q, k, v, batch, seq):
    n, att_dim = q.shape
    n_heads = att_dim // HEAD_DIM
    qblocks = seq // TQ
    qspec = pl.BlockSpec((TQ, att_dim), lambda b, i: (b * qblocks + i, 0))
    kvspec = pl.BlockSpec((seq, att_dim), lambda b, i: (b, 0))
    return pl.pallas_call(
        _attn_kernel,
        grid=(batch, qblocks),
        in_specs=[qspec, kvspec, kvspec],
        out_specs=qspec,
        out_shape=jax.ShapeDtypeStruct((n, att_dim), BF16),
        scratch_shapes=[pltpu.VMEM((TQ, att_dim), F32),
                        pltpu.VMEM((n_heads * TQ, LANES), F32),
                        pltpu.VMEM((n_heads * TQ, MXU_DIM), BF16),
                        pltpu.VMEM((n_heads, seq, MXU_DIM), BF16),
                        pltpu.SMEM((1,), jnp.int32)],
        compiler_params=pltpu.CompilerParams(
            dimension_semantics=("arbitrary", "arbitrary"), vmem_limit_bytes=VMEM_LIMIT_BYTES),
        name="stickbreak_attn",
    )(q, k, v)


def _mix_ffn_kernel(y_ref, att_ref, gc_ref, ga_ref, x_ref, lng_ref, lnb_ref, wc32_ref, bc_ref, wa32_ref,
                    wo32_ref, gmix_ref, gpre_ref, wup_ref, wdn_ref, gffn_ref, o_ref, wc_ref, wa_ref, wo_ref,
                    *, d_ff, chunk):
    for src_ref, dst_ref in ((wc32_ref, wc_ref), (wa32_ref, wa_ref), (wo32_ref, wo_ref)):
        _cast_once(src_ref, dst_ref)
    y = y_ref[...]
    yc = y - jnp.mean(y, axis=-1, keepdims=True)
    y = yc * lax.rsqrt(jnp.mean(yc * yc, axis=-1, keepdims=True) + EPS) * lng_ref[...] + lnb_ref[...]
    y = (y * jax.nn.sigmoid(y)).astype(BF16)
    conv_out = jnp.dot(y, wc_ref[...], preferred_element_type=F32) + bc_ref[...]
    att_out = jnp.dot(att_ref[...], wa_ref[...], preferred_element_type=F32)
    merged = gc_ref[...].astype(F32) * conv_out + ga_ref[...].astype(F32) * att_out
    mix = jnp.dot(merged.astype(BF16), wo_ref[...], preferred_element_type=F32)
    x1 = x_ref[...] + _rms(mix, gmix_ref[...])
    h = _rms(x1, gpre_ref[...]).astype(BF16)
    ff = None
    for c0 in range(0, d_ff, chunk):
        width = min(chunk, d_ff - c0)
        gate = jnp.dot(h, wup_ref[:, c0:c0 + width], preferred_element_type=F32)
        up = jnp.dot(h, wup_ref[:, d_ff + c0:d_ff + c0 + width], preferred_element_type=F32)
        act = (gate * jax.nn.sigmoid(gate) * up).astype(BF16)
        part = jnp.dot(act, wdn_ref[c0:c0 + width, :], preferred_element_type=F32)
        ff = part if ff is None else ff + part
    o_ref[...] = x1 + _rms(ff, gffn_ref[...])


def _mix_ffn(y, att, gc, ga, x2, lng, lnb, wc, bc, wa, wo, gmix, gpre, wup, wdn, gffn):
    n, d = x2.shape
    row = lambda width: pl.BlockSpec((TM_FFN, width), lambda i: (i, 0))
    consts = (lng, lnb, wc, bc, wa, wo, gmix, gpre, wup, wdn, gffn)
    return pl.pallas_call(
        functools.partial(_mix_ffn_kernel, d_ff=wdn.shape[0], chunk=FFN_CHUNK),
        grid=(n // TM_FFN,),
        in_specs=[row(y.shape[1]), row(att.shape[1]), row(d), row(d), row(d)]
                 + [_const_spec(a.shape) for a in consts],
        out_specs=row(d),
        out_shape=jax.ShapeDtypeStruct((n, d), F32),
        scratch_shapes=[pltpu.VMEM(a.shape, BF16) for a in (wc, wa, wo)],
        compiler_params=pltpu.CompilerParams(
            dimension_semantics=("arbitrary",), vmem_limit_bytes=VMEM_LIMIT_BYTES),
        name="mix_ffn",
    )(y, att, gc, ga, x2, *consts)


def kernel(x, norm_mix_pre, w_in, conv_dw_w, conv_dw_b, conv_ln_g, conv_ln_b, w_conv_branch, b_conv_branch,
           w_att_branch, w_out, norm_mix_post, norm_ffn_pre, w_ffn_up, w_ffn_down, norm_ffn_post):
    b, s, d = x.shape
    conv_dim = conv_dw_w.shape[1]
    att_dim = w_att_branch.shape[0]
    assert att_dim % MXU_DIM == 0 and s % TQ == 0 and s % TM_PROJ == 0 and (b * s) % TM_FFN == 0
    assert conv_dw_w.shape[0] == CONV_WIDTH and w_in.shape[1] == 2 * conv_dim + 3 * att_dim + 2 * d
    assert HALO >= CONV_WIDTH - 1 - (CONV_WIDTH - 1) % SUBLANES + SUBLANES
    rowvec = lambda a: a.reshape(1, -1).astype(F32)

    x2 = x.reshape(b * s, d)
    y, q, k, v, gc, ga = _inproj(x2, rowvec(norm_mix_pre), w_in.astype(F32), conv_dw_w.astype(F32),
                                 rowvec(conv_dw_b), att_dim, s)
    att = _attention(q, k, v, b, s)
    out = _mix_ffn(y, att, gc, ga, x2, rowvec(conv_ln_g), rowvec(conv_ln_b), w_conv_branch.astype(F32),
                   rowvec(b_conv_branch), w_att_branch.astype(F32), w_out.astype(F32), rowvec(norm_mix_post),
                   rowvec(norm_ffn_pre), w_ffn_up.astype(BF16), w_ffn_down.astype(BF16), rowvec(norm_ffn_post))
    return out.reshape(b, s, d)
```

```python
import functools

import jax
import jax.numpy as jnp
from jax import lax
from jax.experimental import pallas as pl
from jax.experimental.pallas import tpu as pltpu

F32 = jnp.float32
BF16 = jnp.bfloat16

CONV_WIDTH = 31
HEAD_DIM = 64
EPS = 1e-6

SUBLANES = 8
LANES = 128
MXU_DIM = 256
VMEM_LIMIT_BYTES = 56 * 1024 * 1024

TM_PROJ = 512
TM_FFN = 512
TQ = 256
HEADS_PER_GROUP = MXU_DIM // HEAD_DIM
SLAB_ROWS = 2 * TQ
TILES_PER_STEP = 2
HALO = 32
CONV_ROWS = 64
PIECE = 256
FFN_CHUNK = 512
LOG2E = 1.4426950408889634
EXIT_LOG2 = 160.0
SOFTPLUS_CLAMP = 120.0


def _const_spec(shape):
    return pl.BlockSpec(shape, lambda *_: (0,) * len(shape), pipeline_mode=pl.Buffered(1))


def _rms(x, g):
    return x * lax.rsqrt(jnp.mean(x * x, axis=-1, keepdims=True) + EPS) * g


def _exact_zero_like(x):
    bits = lax.bitcast_convert_type(x, jnp.int32)
    return lax.shift_right_logical(lax.shift_right_logical(bits, 16), 16).astype(F32)


def _conv_chunk(ext_ref, taps, bias, r0, lanes):
    rows = CONV_ROWS + SUBLANES
    acc = jnp.broadcast_to(bias, (CONV_ROWS, bias.shape[1]))
    for r in range(SUBLANES):
        part = None
        for d in range(r, CONV_WIDTH, SUBLANES):
            start = r0 + HALO - SUBLANES - (d - r)
            term = taps[CONV_WIDTH - 1 - d] * ext_ref[start:start + rows, lanes]
            part = term if part is None else part + term
        if r:
            part = pltpu.roll(part, r, axis=0)
        acc = acc + part[SUBLANES:, :]
    return acc


def _cast_once(src_ref, dst_ref):
    @pl.when(pl.program_id(0) == 0)
    def _():
        for c0 in range(0, src_ref.shape[1], PIECE):
            dst_ref[:, c0:c0 + PIECE] = src_ref[:, c0:c0 + PIECE].astype(BF16)


def _inproj_kernel(x_ref, g_ref, w32_ref, cw_ref, cb_ref,
                   y_ref, q_ref, k_ref, v_ref, gc_ref, ga_ref, ext_ref, w_ref,
                   *, conv_dim, att_dim, d_model, tiles_per_seq):
    _cast_once(w32_ref, w_ref)
    h = _rms(x_ref[...], g_ref[...]).astype(BF16)

    def proj(c0, width):
        return jnp.dot(h, w_ref[:, c0:c0 + width], preferred_element_type=F32)

    first = (pl.program_id(0) % tiles_per_seq) == 0

    @pl.when(first)
    def _():
        ext_ref[0:HALO, :] = jnp.zeros((HALO, conv_dim), F32)

    @pl.when(jnp.logical_not(first))
    def _():
        ext_ref[0:HALO, :] = ext_ref[TM_PROJ:TM_PROJ + HALO, :]

    def glu_piece(o):
        u = proj(o, PIECE) * jax.nn.sigmoid(proj(conv_dim + o, PIECE))
        ext_ref[HALO:HALO + TM_PROJ, o:o + PIECE] = u
        return _exact_zero_like(u[0:1, 0:1])

    def plain_piece(ref, o, col, scale, gate):
        p = proj(col, PIECE)
        if scale is not None:
            p = p * scale
        if gate:
            p = jax.nn.sigmoid(p)
        ref[:, o:o + PIECE] = p.astype(BF16)
        return _exact_zero_like(p[0:1, 0:1])

    groups = list(range(0, conv_dim, PIECE))
    glu_later = [functools.partial(glu_piece, o) for o in groups[1:]]
    plain = []
    c = 2 * conv_dim
    for ref, scale in ((q_ref, LOG2E * HEAD_DIM ** -0.5), (k_ref, None), (v_ref, None)):
        for o in range(0, att_dim, PIECE):
            plain.append(functools.partial(plain_piece, ref, o, c + o, scale, False))
        c += att_dim
    for ref in (gc_ref, ga_ref):
        for o in range(0, d_model, PIECE):
            plain.append(functools.partial(plain_piece, ref, o, c + o, None, True))
        c += d_model
    chunks = list(range(0, TM_PROJ, CONV_ROWS))
    share = -(-len(plain) // len(groups))
    edge = glu_piece(groups[0])
    for gi, o in enumerate(groups):
        lanes = slice(o, o + PIECE)
        taps = [cw_ref[j:j + 1, lanes] for j in range(CONV_WIDTH)]
        pieces = glu_later[gi:gi + 1] + plain[gi * share:(gi + 1) * share]
        for idx, r0 in enumerate(chunks):
            y_ref[r0:r0 + CONV_ROWS, lanes] = _conv_chunk(ext_ref, taps, cb_ref[:, lanes] + edge, r0, lanes)
            lo, hi = idx * len(pieces) // len(chunks), (idx + 1) * len(pieces) // len(chunks)
            for piece in pieces[lo:hi]:
                edge = piece()


def _inproj(x2, g, w_in, cw, cb, att_dim, seq):
    n, d = x2.shape
    conv_dim = cw.shape[1]
    row = lambda width: pl.BlockSpec((TM_PROJ, width), lambda i: (i, 0))
    return pl.pallas_call(
        functools.partial(_inproj_kernel, conv_dim=conv_dim, att_dim=att_dim, d_model=d,
                          tiles_per_seq=seq // TM_PROJ),
        grid=(n // TM_PROJ,),
        in_specs=[row(d), _const_spec((1, d)), _const_spec(w_in.shape),
                  _const_spec(cw.shape), _const_spec(cb.shape)],
        out_specs=[row(conv_dim), row(att_dim), row(att_dim), row(att_dim), row(d), row(d)],
        scratch_shapes=[pltpu.VMEM((HALO + TM_PROJ, conv_dim), F32), pltpu.VMEM(w_in.shape, BF16)],
        out_shape=[
            jax.ShapeDtypeStruct((n, conv_dim), F32),
            jax.ShapeDtypeStruct((n, att_dim), BF16),
            jax.ShapeDtypeStruct((n, att_dim), BF16),
            jax.ShapeDtypeStruct((n, att_dim), BF16),
            jax.ShapeDtypeStruct((n, d), BF16),
            jax.ShapeDtypeStruct((n, d), BF16),
        ],
        compiler_params=pltpu.CompilerParams(
            dimension_semantics=("arbitrary",), vmem_limit_bytes=VMEM_LIMIT_BYTES),
        name="inproj",
    )(x2, g, w_in, cw, cb)


def _softplus2(z):
    return jnp.maximum(jnp.log2(1.0 + jnp.exp2(jnp.minimum(z, SOFTPLUS_CLAMP))), z)


def _attn_kernel(q_ref, k_ref, v_ref, o_ref, acc_ref, carry_ref, qh_ref, vh_ref, live_ref):
    step = pl.program_id(1)
    n_heads = q_ref.shape[1] // HEAD_DIM
    tile_rows = n_heads * TQ
    lane_head = lax.broadcasted_iota(jnp.int32, (1, MXU_DIM), 1) // HEAD_DIM
    head_masks = [lane_head == hh for hh in range(HEADS_PER_GROUP)]
    group_lanes = lambda head: slice(head // HEADS_PER_GROUP * MXU_DIM, (head // HEADS_PER_GROUP + 1) * MXU_DIM)

    @pl.when(step == 0)
    def _():
        for head in range(n_heads):
            for r0 in range(0, vh_ref.shape[1], TQ):
                vb = v_ref[r0:r0 + TQ, group_lanes(head)]
                vh_ref[head, r0:r0 + TQ, :] = jnp.where(head_masks[head % HEADS_PER_GROUP], vb, jnp.zeros_like(vb))

    for tile in range(TILES_PER_STEP):
        for head in range(n_heads):
            q = q_ref[tile * TQ:(tile + 1) * TQ, group_lanes(head)]
            r0 = tile * tile_rows + head * TQ
            qh_ref[r0:r0 + TQ, :] = jnp.where(head_masks[head % HEADS_PER_GROUP], q, jnp.zeros_like(q))

    key_idx = lax.broadcasted_iota(jnp.int32, (TQ, TQ), 0)
    col_idx = lax.broadcasted_iota(jnp.int32, (TQ, TQ), 1)
    suffix_ones = (key_idx >= col_idx).astype(BF16)
    half = TQ // 2
    below_diag = (lax.broadcasted_iota(jnp.int32, (half, half), 1)
                  < lax.broadcasted_iota(jnp.int32, (half, half), 0))

    def causal_quadrants(fn):
        lo, hi = slice(0, half), slice(half, TQ)
        out = []
        for h0 in range(0, SLAB_ROWS, TQ):
            top, bottom = slice(h0, h0 + half), slice(h0 + half, h0 + TQ)
            upper_left = jnp.where(below_diag, fn(top, lo), 0.0)
            lower_right = jnp.where(below_diag, fn(bottom, hi), 0.0)
            out.append(jnp.concatenate([upper_left, jnp.zeros_like(upper_left)], axis=1))
            out.append(jnp.concatenate([fn(bottom, lo), lower_right], axis=1))
        return jnp.concatenate(out, axis=0)

    acc_ref[...] = jnp.zeros_like(acc_ref)
    carry_ref[...] = jnp.zeros_like(carry_ref)

    def block(tile, k0, diag):
        out_rows = slice(tile * TQ, (tile + 1) * TQ)
        slowest = None
        for r0 in range(0, tile_rows, SLAB_ROWS):
            sl = slice(tile * tile_rows + r0, tile * tile_rows + r0 + SLAB_ROWS)
            lanes = group_lanes(r0 // TQ)
            z = lax.dot_general(qh_ref[sl, :], k_ref[pl.ds(k0, TQ), lanes], (((1,), (1,)), ((), ())),
                                preferred_element_type=F32)
            sp = causal_quadrants(lambda r, c: _softplus2(z[r, c])) if diag else _softplus2(z)
            suffix = jnp.dot(sp.astype(BF16), suffix_ones, preferred_element_type=F32)
            carry = carry_ref[sl, :]
            if diag:
                a = causal_quadrants(lambda r, c: jnp.exp2(z[r, c] - suffix[r, c] + carry[r, :]))
            else:
                a = jnp.exp2(z - suffix + jnp.concatenate([carry] * (TQ // LANES), axis=1))
            a = a.astype(BF16)
            acc = acc_ref[out_rows, lanes]
            for h0 in range(0, SLAB_ROWS, TQ):
                acc += jnp.dot(a[h0:h0 + TQ, :], vh_ref[(r0 + h0) // TQ, pl.ds(k0, TQ), :],
                               preferred_element_type=F32)
            acc_ref[out_rows, lanes] = acc
            carry = carry - jnp.broadcast_to(suffix[:, 0:1], carry.shape)
            carry_ref[sl, :] = carry
            slowest = carry if slowest is None else jnp.maximum(slowest, carry)
        return (jnp.max(slowest) > -EXIT_LOG2).astype(jnp.int32)

    def key_block(tile):
        return TILES_PER_STEP * step + tile

    def key_start(tile, back):
        return pl.multiple_of((key_block(tile) - back) * TQ, TQ)

    @pl.when(step == 0)
    def _():
        live_ref[0] = block(0, key_start(0, 0), True)
        for tile in range(1, TILES_PER_STEP):
            block(tile, key_start(tile, 0), True)
        for tile in range(1, TILES_PER_STEP):
            live_ref[tile] = block(tile, key_start(tile, 1), False)

    @pl.when(step > 0)
    def _():
        for tile in range(TILES_PER_STEP):
            block(tile, key_start(tile, 0), True)
        for tile in range(TILES_PER_STEP):
            live_ref[tile] = block(tile, key_start(tile, 1), False)

    for tile in range(TILES_PER_STEP):
        def cond(state, tile=tile):
            back, live = state
            return jnp.logical_and(back <= key_block(tile), live > 0)

        def body(state, tile=tile):
            back, _ = state
            return back + 1, block(tile, key_start(tile, back), False)

        lax.while_loop(cond, body, (jnp.int32(2), live_ref[tile]))
    o_ref[...] = acc_ref[...].astype(o_ref.dtype)


def _attention(q, k, v, batch, seq):
    n, att_dim = q.shape
    n_heads = att_dim // HEAD_DIM
    steps = seq // (TILES_PER_STEP * TQ)
    qspec = pl.BlockSpec((TILES_PER_STEP * TQ, att_dim), lambda b, i: (b * steps + i, 0))
    kvspec = pl.BlockSpec((seq, att_dim), lambda b, i: (b, 0))
    return pl.pallas_call(
        _attn_kernel,
        grid=(batch, steps),
        in_specs=[qspec, kvspec, kvspec],
        out_specs=qspec,
        out_shape=jax.ShapeDtypeStruct((n, att_dim), BF16),
        scratch_shapes=[pltpu.VMEM((TILES_PER_STEP * TQ, att_dim), F32),
                        pltpu.VMEM((TILES_PER_STEP * n_heads * TQ, LANES), F32),
                        pltpu.VMEM((TILES_PER_STEP * n_heads * TQ, MXU_DIM), BF16),
                        pltpu.VMEM((n_heads, seq, MXU_DIM), BF16),
                        pltpu.SMEM((TILES_PER_STEP,), jnp.int32)],
        compiler_params=pltpu.CompilerParams(
            dimension_semantics=("arbitrary", "arbitrary"), vmem_limit_bytes=VMEM_LIMIT_BYTES),
        name="stickbreak_attn",
    )(q, k, v)


def _mix_ffn_kernel(y_ref, att_ref, gc_ref, ga_ref, x_ref, lng_ref, lnb_ref, wc32_ref, bc_ref, wa32_ref,
                    wo32_ref, gmix_ref, gpre_ref, wup_ref, wdn_ref, gffn_ref, o_ref, wc_ref, wa_ref, wo_ref,
                    *, d_ff, chunk):
    for src_ref, dst_ref in ((wc32_ref, wc_ref), (wa32_ref, wa_ref), (wo32_ref, wo_ref)):
        _cast_once(src_ref, dst_ref)
    y = y_ref[...]
    yc = y - jnp.mean(y, axis=-1, keepdims=True)
    y = yc * lax.rsqrt(jnp.mean(yc * yc, axis=-1, keepdims=True) + EPS) * lng_ref[...] + lnb_ref[...]
    y = (y * jax.nn.sigmoid(y)).astype(BF16)
    conv_out = jnp.dot(y, wc_ref[...], preferred_element_type=F32) + bc_ref[...]
    att_out = jnp.dot(att_ref[...], wa_ref[...], preferred_element_type=F32)
    merged = gc_ref[...].astype(F32) * conv_out + ga_ref[...].astype(F32) * att_out
    mix = jnp.dot(merged.astype(BF16), wo_ref[...], preferred_element_type=F32)
    x1 = x_ref[...] + _rms(mix, gmix_ref[...])
    h = _rms(x1, gpre_ref[...]).astype(BF16)
    ff = None
    for c0 in range(0, d_ff, chunk):
        width = min(chunk, d_ff - c0)
        gate = jnp.dot(h, wup_ref[:, c0:c0 + width], preferred_element_type=F32)
        up = jnp.dot(h, wup_ref[:, d_ff + c0:d_ff + c0 + width], preferred_element_type=F32)
        act = (gate * jax.nn.sigmoid(gate) * up).astype(BF16)
        part = jnp.dot(act, wdn_ref[c0:c0 + width, :], preferred_element_type=F32)
        ff = part if ff is None else ff + part
    o_ref[...] = x1 + _rms(ff, gffn_ref[...])


def _mix_ffn(y, att, gc, ga, x2, lng, lnb, wc, bc, wa, wo, gmix, gpre, wup, wdn, gffn):
    n, d = x2.shape
    row = lambda width: pl.BlockSpec((TM_FFN, width), lambda i: (i, 0))
    consts = (lng, lnb, wc, bc, wa, wo, gmix, gpre, wup, wdn, gffn)
    return pl.pallas_call(
        functools.partial(_mix_ffn_kernel, d_ff=wdn.shape[0], chunk=FFN_CHUNK),
        grid=(n // TM_FFN,),
        in_specs=[row(y.shape[1]), row(att.shape[1]), row(d), row(d), row(d)]
                 + [_const_spec(a.shape) for a in consts],
        out_specs=row(d),
        out_shape=jax.ShapeDtypeStruct((n, d), F32),
        scratch_shapes=[pltpu.VMEM(a.shape, BF16) for a in (wc, wa, wo)],
        compiler_params=pltpu.CompilerParams(
            dimension_semantics=("arbitrary",), vmem_limit_bytes=VMEM_LIMIT_BYTES),
        name="mix_ffn",
    )(y, att, gc, ga, x2, *consts)


def kernel(x, norm_mix_pre, w_in, conv_dw_w, conv_dw_b, conv_ln_g, conv_ln_b, w_conv_branch, b_conv_branch,
           w_att_branch, w_out, norm_mix_post, norm_ffn_pre, w_ffn_up, w_ffn_down, norm_ffn_post):
    b, s, d = x.shape
    conv_dim = conv_dw_w.shape[1]
    att_dim = w_att_branch.shape[0]
    assert att_dim % MXU_DIM == 0 and s % (TILES_PER_STEP * TQ) == 0 and TQ == 2 * LANES
    assert s % TM_PROJ == 0 and (b * s) % TM_FFN == 0
    assert conv_dw_w.shape[0] == CONV_WIDTH and w_in.shape[1] == 2 * conv_dim + 3 * att_dim + 2 * d
    assert HALO >= CONV_WIDTH - 1 - (CONV_WIDTH - 1) % SUBLANES + SUBLANES
    rowvec = lambda a: a.reshape(1, -1).astype(F32)

    x2 = x.reshape(b * s, d)
    y, q, k, v, gc, ga = _inproj(x2, rowvec(norm_mix_pre), w_in.astype(F32), conv_dw_w.astype(F32),
                                 rowvec(conv_dw_b), att_dim, s)
    att = _attention(q, k, v, b, s)
    out = _mix_ffn(y, att, gc, ga, x2, rowvec(conv_ln_g), rowvec(conv_ln_b), w_conv_branch.astype(F32),
                   rowvec(b_conv_branch), w_att_branch.astype(F32), w_out.astype(F32), rowvec(norm_mix_post),
                   rowvec(norm_ffn_pre), w_ffn_up.astype(BF16), w_ffn_down.astype(BF16), rowvec(norm_ffn_post))
    return out.reshape(b, s, d)
```

```python
import functools

import jax
import jax.numpy as jnp
from jax import lax
from jax.experimental import pallas as pl
from jax.experimental.pallas import tpu as pltpu

F32 = jnp.float32
BF16 = jnp.bfloat16

CONV_WIDTH = 31
HEAD_DIM = 64
EPS = 1e-6

SUBLANES = 8
LANES = 128
MXU_DIM = 256
VMEM_LIMIT_BYTES = 56 * 1024 * 1024

TM_PROJ = 512
TM_FFN = 512
TQ = 256
HEADS_PER_GROUP = MXU_DIM // HEAD_DIM
SLAB_ROWS = 2 * TQ
TILES_PER_STEP = 4
HALO = 32
CONV_ROWS = 64
PIECE = 256
FFN_CHUNK = 512
LOG2E = 1.4426950408889634
EXIT_LOG2 = 160.0
SOFTPLUS_CLAMP = 120.0


def _const_spec(shape):
    return pl.BlockSpec(shape, lambda *_: (0,) * len(shape), pipeline_mode=pl.Buffered(1))


def _rms(x, g):
    return x * lax.rsqrt(jnp.mean(x * x, axis=-1, keepdims=True) + EPS) * g


def _exact_zero_like(x):
    bits = lax.bitcast_convert_type(x, jnp.int32)
    return lax.shift_right_logical(lax.shift_right_logical(bits, 16), 16).astype(F32)


def _conv_chunk(ext_ref, taps, bias, r0, lanes):
    rows = CONV_ROWS + SUBLANES
    acc = jnp.broadcast_to(bias, (CONV_ROWS, bias.shape[1]))
    for r in range(SUBLANES):
        part = None
        for d in range(r, CONV_WIDTH, SUBLANES):
            start = r0 + HALO - SUBLANES - (d - r)
            term = taps[CONV_WIDTH - 1 - d] * ext_ref[start:start + rows, lanes]
            part = term if part is None else part + term
        if r:
            part = pltpu.roll(part, r, axis=0)
        acc = acc + part[SUBLANES:, :]
    return acc


def _cast_once(src_ref, dst_ref):
    @pl.when(pl.program_id(0) == 0)
    def _():
        for c0 in range(0, src_ref.shape[1], PIECE):
            dst_ref[:, c0:c0 + PIECE] = src_ref[:, c0:c0 + PIECE].astype(BF16)


def _inproj_kernel(x_ref, g_ref, w32_ref, cw_ref, cb_ref,
                   y_ref, q_ref, k_ref, v_ref, gc_ref, ga_ref, ext_ref, w_ref,
                   *, conv_dim, att_dim, d_model, tiles_per_seq):
    _cast_once(w32_ref, w_ref)
    h = _rms(x_ref[...], g_ref[...]).astype(BF16)

    def proj(c0, width):
        return jnp.dot(h, w_ref[:, c0:c0 + width], preferred_element_type=F32)

    first = (pl.program_id(0) % tiles_per_seq) == 0

    @pl.when(first)
    def _():
        ext_ref[0:HALO, :] = jnp.zeros((HALO, conv_dim), F32)

    @pl.when(jnp.logical_not(first))
    def _():
        ext_ref[0:HALO, :] = ext_ref[TM_PROJ:TM_PROJ + HALO, :]

    def glu_piece(o):
        u = proj(o, PIECE) * jax.nn.sigmoid(proj(conv_dim + o, PIECE))
        ext_ref[HALO:HALO + TM_PROJ, o:o + PIECE] = u
        return _exact_zero_like(u[0:1, 0:1])

    def plain_piece(ref, o, col, scale, gate):
        p = proj(col, PIECE)
        if scale is not None:
            p = p * scale
        if gate:
            p = jax.nn.sigmoid(p)
        ref[:, o:o + PIECE] = p.astype(BF16)
        return _exact_zero_like(p[0:1, 0:1])

    groups = list(range(0, conv_dim, PIECE))
    glu_later = [functools.partial(glu_piece, o) for o in groups[1:]]
    plain = []
    c = 2 * conv_dim
    for ref, scale in ((q_ref, LOG2E * HEAD_DIM ** -0.5), (k_ref, None), (v_ref, None)):
        for o in range(0, att_dim, PIECE):
            plain.append(functools.partial(plain_piece, ref, o, c + o, scale, False))
        c += att_dim
    for ref in (gc_ref, ga_ref):
        for o in range(0, d_model, PIECE):
            plain.append(functools.partial(plain_piece, ref, o, c + o, None, True))
        c += d_model
    chunks = list(range(0, TM_PROJ, CONV_ROWS))
    share = -(-len(plain) // len(groups))
    edge = glu_piece(groups[0])
    for gi, o in enumerate(groups):
        lanes = slice(o, o + PIECE)
        taps = [cw_ref[j:j + 1, lanes] for j in range(CONV_WIDTH)]
        pieces = glu_later[gi:gi + 1] + plain[gi * share:(gi + 1) * share]
        for idx, r0 in enumerate(chunks):
            y_ref[r0:r0 + CONV_ROWS, lanes] = _conv_chunk(ext_ref, taps, cb_ref[:, lanes] + edge, r0, lanes)
            lo, hi = idx * len(pieces) // len(chunks), (idx + 1) * len(pieces) // len(chunks)
            for piece in pieces[lo:hi]:
                edge = piece()


def _inproj(x2, g, w_in, cw, cb, att_dim, seq):
    n, d = x2.shape
    conv_dim = cw.shape[1]
    row = lambda width: pl.BlockSpec((TM_PROJ, width), lambda i: (i, 0))
    return pl.pallas_call(
        functools.partial(_inproj_kernel, conv_dim=conv_dim, att_dim=att_dim, d_model=d,
                          tiles_per_seq=seq // TM_PROJ),
        grid=(n // TM_PROJ,),
        in_specs=[row(d), _const_spec((1, d)), _const_spec(w_in.shape),
                  _const_spec(cw.shape), _const_spec(cb.shape)],
        out_specs=[row(conv_dim), row(att_dim), row(att_dim), row(att_dim), row(d), row(d)],
        scratch_shapes=[pltpu.VMEM((HALO + TM_PROJ, conv_dim), F32), pltpu.VMEM(w_in.shape, BF16)],
        out_shape=[
            jax.ShapeDtypeStruct((n, conv_dim), F32),
            jax.ShapeDtypeStruct((n, att_dim), BF16),
            jax.ShapeDtypeStruct((n, att_dim), BF16),
            jax.ShapeDtypeStruct((n, att_dim), BF16),
            jax.ShapeDtypeStruct((n, d), BF16),
            jax.ShapeDtypeStruct((n, d), BF16),
        ],
        compiler_params=pltpu.CompilerParams(
            dimension_semantics=("arbitrary",), vmem_limit_bytes=VMEM_LIMIT_BYTES),
        name="inproj",
    )(x2, g, w_in, cw, cb)


def _softplus2(z):
    return jnp.maximum(jnp.log2(1.0 + jnp.exp2(jnp.minimum(z, SOFTPLUS_CLAMP))), z)


def _attn_kernel(q_ref, k_ref, v_ref, o_ref, acc_ref, carry_ref, qh_ref, vh_ref, live_ref):
    step = pl.program_id(1)
    n_heads = q_ref.shape[1] // HEAD_DIM
    tile_rows = n_heads * TQ
    lane_head = lax.broadcasted_iota(jnp.int32, (1, MXU_DIM), 1) // HEAD_DIM
    head_masks = [lane_head == hh for hh in range(HEADS_PER_GROUP)]
    group_lanes = lambda head: slice(head // HEADS_PER_GROUP * MXU_DIM, (head // HEADS_PER_GROUP + 1) * MXU_DIM)

    @pl.when(step == 0)
    def _():
        for head in range(n_heads):
            for r0 in range(0, vh_ref.shape[1], TQ):
                vb = v_ref[r0:r0 + TQ, group_lanes(head)]
                vh_ref[head, r0:r0 + TQ, :] = jnp.where(head_masks[head % HEADS_PER_GROUP], vb, jnp.zeros_like(vb))

    for tile in range(TILES_PER_STEP):
        for head in range(n_heads):
            q = q_ref[tile * TQ:(tile + 1) * TQ, group_lanes(head)]
            r0 = tile * tile_rows + head * TQ
            qh_ref[r0:r0 + TQ, :] = jnp.where(head_masks[head % HEADS_PER_GROUP], q, jnp.zeros_like(q))

    key_idx = lax.broadcasted_iota(jnp.int32, (TQ, TQ), 0)
    col_idx = lax.broadcasted_iota(jnp.int32, (TQ, TQ), 1)
    suffix_ones = (key_idx >= col_idx).astype(BF16)
    half = TQ // 2
    below_diag = (lax.broadcasted_iota(jnp.int32, (half, half), 1)
                  < lax.broadcasted_iota(jnp.int32, (half, half), 0))

    def causal_quadrants(fn):
        lo, hi = slice(0, half), slice(half, TQ)
        out = []
        for h0 in range(0, SLAB_ROWS, TQ):
            top, bottom = slice(h0, h0 + half), slice(h0 + half, h0 + TQ)
            upper_left = jnp.where(below_diag, fn(top, lo), 0.0)
            lower_right = jnp.where(below_diag, fn(bottom, hi), 0.0)
            out.append(jnp.concatenate([upper_left, jnp.zeros_like(upper_left)], axis=1))
            out.append(jnp.concatenate([fn(bottom, lo), lower_right], axis=1))
        return jnp.concatenate(out, axis=0)

    acc_ref[...] = jnp.zeros_like(acc_ref)
    carry_ref[...] = jnp.zeros_like(carry_ref)

    def block(tile, k0, diag):
        out_rows = slice(tile * TQ, (tile + 1) * TQ)
        slowest = None
        for r0 in range(0, tile_rows, SLAB_ROWS):
            sl = slice(tile * tile_rows + r0, tile * tile_rows + r0 + SLAB_ROWS)
            lanes = group_lanes(r0 // TQ)
            z = lax.dot_general(qh_ref[sl, :], k_ref[pl.ds(k0, TQ), lanes], (((1,), (1,)), ((), ())),
                                preferred_element_type=F32)
            sp = causal_quadrants(lambda r, c: _softplus2(z[r, c])) if diag else _softplus2(z)
            suffix = jnp.dot(sp.astype(BF16), suffix_ones, preferred_element_type=F32)
            carry = carry_ref[sl, :]
            if diag:
                a = causal_quadrants(lambda r, c: jnp.exp2(z[r, c] - suffix[r, c] + carry[r, :]))
            else:
                a = jnp.exp2(z - suffix + jnp.concatenate([carry] * (TQ // LANES), axis=1))
            a = a.astype(BF16)
            acc = acc_ref[out_rows, lanes]
            for h0 in range(0, SLAB_ROWS, TQ):
                acc += jnp.dot(a[h0:h0 + TQ, :], vh_ref[(r0 + h0) // TQ, pl.ds(k0, TQ), :],
                               preferred_element_type=F32)
            acc_ref[out_rows, lanes] = acc
            carry = carry - jnp.broadcast_to(suffix[:, 0:1], carry.shape)
            carry_ref[sl, :] = carry
            slowest = carry if slowest is None else jnp.maximum(slowest, carry)
        return (jnp.max(slowest) > -EXIT_LOG2).astype(jnp.int32)

    def key_block(tile):
        return TILES_PER_STEP * step + tile

    def key_start(tile, back):
        return pl.multiple_of((key_block(tile) - back) * TQ, TQ)

    @pl.when(step == 0)
    def _():
        live_ref[0] = block(0, key_start(0, 0), True)
        for tile in range(1, TILES_PER_STEP):
            block(tile, key_start(tile, 0), True)
        for tile in range(1, TILES_PER_STEP):
            live_ref[tile] = block(tile, key_start(tile, 1), False)

    @pl.when(step > 0)
    def _():
        for tile in range(TILES_PER_STEP):
            block(tile, key_start(tile, 0), True)
        for tile in range(TILES_PER_STEP):
            live_ref[tile] = block(tile, key_start(tile, 1), False)

    for tile in range(TILES_PER_STEP):
        def cond(state, tile=tile):
            back, live = state
            return jnp.logical_and(back <= key_block(tile), live > 0)

        def body(state, tile=tile):
            back, _ = state
            return back + 1, block(tile, key_start(tile, back), False)

        lax.while_loop(cond, body, (jnp.int32(2), live_ref[tile]))
    o_ref[...] = acc_ref[...].astype(o_ref.dtype)


def _attention(q, k, v, batch, seq):
    n, att_dim = q.shape
    n_heads = att_dim // HEAD_DIM
    steps = seq // (TILES_PER_STEP * TQ)
    qspec = pl.BlockSpec((TILES_PER_STEP * TQ, att_dim), lambda b, i: (b * steps + i, 0))
    kvspec = pl.BlockSpec((seq, att_dim), lambda b, i: (b, 0))
    return pl.pallas_call(
        _attn_kernel,
        grid=(batch, steps),
        in_specs=[qspec, kvspec, kvspec],
        out_specs=qspec,
        out_shape=jax.ShapeDtypeStruct((n, att_dim), BF16),
        scratch_shapes=[pltpu.VMEM((TILES_PER_STEP * TQ, att_dim), F32),
                        pltpu.VMEM((TILES_PER_STEP * n_heads * TQ, LANES), F32),
                        pltpu.VMEM((TILES_PER_STEP * n_heads * TQ, MXU_DIM), BF16),
                        pltpu.VMEM((n_heads, seq, MXU_DIM), BF16),
                        pltpu.SMEM((TILES_PER_STEP,), jnp.int32)],
        compiler_params=pltpu.CompilerParams(
            dimension_semantics=("arbitrary", "arbitrary"), vmem_limit_bytes=VMEM_LIMIT_BYTES),
        name="stickbreak_attn",
    )(q, k, v)


def _mix_ffn_kernel(y_ref, att_ref, gc_ref, ga_ref, x_ref, lng_ref, lnb_ref, wc32_ref, bc_ref, wa32_ref,
                    wo32_ref, gmix_ref, gpre_ref, wup_ref, wdn_ref, gffn_ref, o_ref, wc_ref, wa_ref, wo_ref,
                    *, d_ff, chunk):
    for src_ref, dst_ref in ((wc32_ref, wc_ref), (wa32_ref, wa_ref), (wo32_ref, wo_ref)):
        _cast_once(src_ref, dst_ref)
    att_gated = ga_ref[...].astype(F32) * jnp.dot(att_ref[...], wa_ref[...], preferred_element_type=F32)
    y = y_ref[...]
    yc = y - jnp.mean(y, axis=-1, keepdims=True)
    y = yc * lax.rsqrt(jnp.mean(yc * yc, axis=-1, keepdims=True) + EPS) * lng_ref[...] + lnb_ref[...]
    y = (y * jax.nn.sigmoid(y)).astype(BF16)
    conv_out = jnp.dot(y, wc_ref[...], preferred_element_type=F32) + bc_ref[...]
    merged = gc_ref[...].astype(F32) * conv_out + att_gated
    mix = jnp.dot(merged.astype(BF16), wo_ref[...], preferred_element_type=F32)
    x1 = x_ref[...] + _rms(mix, gmix_ref[...])
    h = _rms(x1, gpre_ref[...]).astype(BF16)
    ff = None
    for c0 in range(0, d_ff, chunk):
        width = min(chunk, d_ff - c0)
        gate = jnp.dot(h, wup_ref[:, c0:c0 + width], preferred_element_type=F32)
        up = jnp.dot(h, wup_ref[:, d_ff + c0:d_ff + c0 + width], preferred_element_type=F32)
        act = (gate * jax.nn.sigmoid(gate) * up).astype(BF16)
        part = jnp.dot(act, wdn_ref[c0:c0 + width, :], preferred_element_type=F32)
        ff = part if ff is None else ff + part
    o_ref[...] = x1 + _rms(ff, gffn_ref[...])


def _mix_ffn(y, att, gc, ga, x2, lng, lnb, wc, bc, wa, wo, gmix, gpre, wup, wdn, gffn):
    n, d = x2.shape
    row = lambda width: pl.BlockSpec((TM_FFN, width), lambda i: (i, 0))
    consts = (lng, lnb, wc, bc, wa, wo, gmix, gpre, wup, wdn, gffn)
    return pl.pallas_call(
        functools.partial(_mix_ffn_kernel, d_ff=wdn.shape[0], chunk=FFN_CHUNK),
        grid=(n // TM_FFN,),
        in_specs=[row(y.shape[1]), row(att.shape[1]), row(d), row(d), row(d)]
                 + [_const_spec(a.shape) for a in consts],
        out_specs=row(d),
        out_shape=jax.ShapeDtypeStruct((n, d), F32),
        scratch_shapes=[pltpu.VMEM(a.shape, BF16) for a in (wc, wa, wo)],
        compiler_params=pltpu.CompilerParams(
            dimension_semantics=("arbitrary",), vmem_limit_bytes=VMEM_LIMIT_BYTES),
        name="mix_ffn",
    )(y, att, gc, ga, x2, *consts)


def kernel(x, norm_mix_pre, w_in, conv_dw_w, conv_dw_b, conv_ln_g, conv_ln_b, w_conv_branch, b_conv_branch,
           w_att_branch, w_out, norm_mix_post, norm_ffn_pre, w_ffn_up, w_ffn_down, norm_ffn_post):
    b, s, d = x.shape
    conv_dim = conv_dw_w.shape[1]
    att_dim = w_att_branch.shape[0]
    assert att_dim % MXU_DIM == 0 and s % (TILES_PER_STEP * TQ) == 0 and TQ == 2 * LANES
    assert s % TM_PROJ == 0 and (b * s) % TM_FFN == 0
    assert conv_dw_w.shape[0] == CONV_WIDTH and w_in.shape[1] == 2 * conv_dim + 3 * att_dim + 2 * d
    assert HALO >= CONV_WIDTH - 1 - (CONV_WIDTH - 1) % SUBLANES + SUBLANES
    rowvec = lambda a: a.reshape(1, -1).astype(F32)

    x2 = x.reshape(b * s, d)
    y, q, k, v, gc, ga = _inproj(x2, rowvec(norm_mix_pre), w_in.astype(F32), conv_dw_w.astype(F32),
                                 rowvec(conv_dw_b), att_dim, s)
    att = _attention(q, k, v, b, s)
    out = _mix_ffn(y, att, gc, ga, x2, rowvec(conv_ln_g), rowvec(conv_ln_b), w_conv_branch.astype(F32),
                   rowvec(b_conv_branch), w_att_branch.astype(F32), w_out.astype(F32), rowvec(norm_mix_post),
                   rowvec(norm_ffn_pre), w_ffn_up.astype(BF16), w_ffn_down.astype(BF16), rowvec(norm_ffn_post))
    return out.reshape(b, s, d)
```

```python
import functools

import jax
import jax.numpy as jnp
from jax import lax
from jax.experimental import pallas as pl
from jax.experimental.pallas import tpu as pltpu

F32 = jnp.float32
BF16 = jnp.bfloat16

CONV_WIDTH = 31
HEAD_DIM = 64
EPS = 1e-6

SUBLANES = 8
LANES = 128
MXU_DIM = 256
VMEM_LIMIT_BYTES = 56 * 1024 * 1024

TM_PROJ = 512
TM_FFN = 512
TQ = 256
HEADS_PER_GROUP = MXU_DIM // HEAD_DIM
SLAB_ROWS = 2 * TQ
TILES_PER_STEP = 2
HALO = 32
CONV_ROWS = 64
PIECE = 256
FFN_CHUNK = 512
NORM_ROWS = 128
LOG2E = 1.4426950408889634
EXIT_LOG2 = 160.0
SOFTPLUS_CLAMP = 120.0


def _const_spec(shape):
    return pl.BlockSpec(shape, lambda *_: (0,) * len(shape), pipeline_mode=pl.Buffered(1))


def _rms(x, g):
    return x * lax.rsqrt(jnp.mean(x * x, axis=-1, keepdims=True) + EPS) * g


def _exact_zero_like(x):
    bits = lax.bitcast_convert_type(x, jnp.int32)
    return lax.shift_right_logical(lax.shift_right_logical(bits, 16), 16).astype(F32)


def _conv_chunk(ext_ref, taps, bias, r0, lanes):
    rows = CONV_ROWS + SUBLANES
    acc = jnp.broadcast_to(bias, (CONV_ROWS, bias.shape[1]))
    for r in range(SUBLANES):
        part = None
        for d in range(r, CONV_WIDTH, SUBLANES):
            start = r0 + HALO - SUBLANES - (d - r)
            term = taps[CONV_WIDTH - 1 - d] * ext_ref[start:start + rows, lanes]
            part = term if part is None else part + term
        if r:
            part = pltpu.roll(part, r, axis=0)
        acc = acc + part[SUBLANES:, :]
    return acc


def _cast_once(src_ref, dst_ref):
    @pl.when(pl.program_id(0) == 0)
    def _():
        for c0 in range(0, src_ref.shape[1], PIECE):
            dst_ref[:, c0:c0 + PIECE] = src_ref[:, c0:c0 + PIECE].astype(BF16)


def _inproj_kernel(x_ref, g_ref, w32_ref, cw_ref, cb_ref,
                   y_ref, q_ref, k_ref, v_ref, gc_ref, ga_ref, ext_ref, w_ref,
                   *, conv_dim, att_dim, d_model, tiles_per_seq):
    _cast_once(w32_ref, w_ref)
    h = _rms(x_ref[...], g_ref[...]).astype(BF16)

    def proj(c0, width):
        return jnp.dot(h, w_ref[:, c0:c0 + width], preferred_element_type=F32)

    first = (pl.program_id(0) % tiles_per_seq) == 0

    @pl.when(first)
    def _():
        ext_ref[0:HALO, :] = jnp.zeros((HALO, conv_dim), F32)

    @pl.when(jnp.logical_not(first))
    def _():
        ext_ref[0:HALO, :] = ext_ref[TM_PROJ:TM_PROJ + HALO, :]

    def glu_piece(o):
        u = proj(o, PIECE) * jax.nn.sigmoid(proj(conv_dim + o, PIECE))
        ext_ref[HALO:HALO + TM_PROJ, o:o + PIECE] = u
        return _exact_zero_like(u[0:1, 0:1])

    def plain_piece(ref, o, col, scale, gate):
        p = proj(col, PIECE)
        if scale is not None:
            p = p * scale
        if gate:
            p = jax.nn.sigmoid(p)
        ref[:, o:o + PIECE] = p.astype(BF16)
        return _exact_zero_like(p[0:1, 0:1])

    groups = list(range(0, conv_dim, PIECE))
    glu_later = [functools.partial(glu_piece, o) for o in groups[1:]]
    plain = []
    c = 2 * conv_dim
    for ref, scale in ((q_ref, LOG2E * HEAD_DIM ** -0.5), (k_ref, None), (v_ref, None)):
        for o in range(0, att_dim, PIECE):
            plain.append(functools.partial(plain_piece, ref, o, c + o, scale, False))
        c += att_dim
    for ref in (gc_ref, ga_ref):
        for o in range(0, d_model, PIECE):
            plain.append(functools.partial(plain_piece, ref, o, c + o, None, True))
        c += d_model
    chunks = list(range(0, TM_PROJ, CONV_ROWS))
    share = -(-len(plain) // len(groups))
    edge = glu_piece(groups[0])
    for gi, o in enumerate(groups):
        lanes = slice(o, o + PIECE)
        taps = [cw_ref[j:j + 1, lanes] for j in range(CONV_WIDTH)]
        pieces = glu_later[gi:gi + 1] + plain[gi * share:(gi + 1) * share]
        for idx, r0 in enumerate(chunks):
            y_ref[r0:r0 + CONV_ROWS, lanes] = _conv_chunk(ext_ref, taps, cb_ref[:, lanes] + edge, r0, lanes)
            lo, hi = idx * len(pieces) // len(chunks), (idx + 1) * len(pieces) // len(chunks)
            for piece in pieces[lo:hi]:
                edge = piece()


def _inproj(x2, g, w_in, cw, cb, att_dim, seq):
    n, d = x2.shape
    conv_dim = cw.shape[1]
    row = lambda width: pl.BlockSpec((TM_PROJ, width), lambda i: (i, 0))
    return pl.pallas_call(
        functools.partial(_inproj_kernel, conv_dim=conv_dim, att_dim=att_dim, d_model=d,
                          tiles_per_seq=seq // TM_PROJ),
        grid=(n // TM_PROJ,),
        in_specs=[row(d), _const_spec((1, d)), _const_spec(w_in.shape),
                  _const_spec(cw.shape), _const_spec(cb.shape)],
        out_specs=[row(conv_dim), row(att_dim), row(att_dim), row(att_dim), row(d), row(d)],
        scratch_shapes=[pltpu.VMEM((HALO + TM_PROJ, conv_dim), F32), pltpu.VMEM(w_in.shape, BF16)],
        out_shape=[
            jax.ShapeDtypeStruct((n, conv_dim), F32),
            jax.ShapeDtypeStruct((n, att_dim), BF16),
            jax.ShapeDtypeStruct((n, att_dim), BF16),
            jax.ShapeDtypeStruct((n, att_dim), BF16),
            jax.ShapeDtypeStruct((n, d), BF16),
            jax.ShapeDtypeStruct((n, d), BF16),
        ],
        compiler_params=pltpu.CompilerParams(
            dimension_semantics=("arbitrary",), vmem_limit_bytes=VMEM_LIMIT_BYTES),
        name="inproj",
    )(x2, g, w_in, cw, cb)


def _softplus2(z):
    return jnp.maximum(jnp.log2(1.0 + jnp.exp2(jnp.minimum(z, SOFTPLUS_CLAMP))), z)


def _attn_kernel(q_ref, k_ref, v_ref, o_ref, acc_ref, carry_ref, qh_ref, vh_ref, live_ref):
    step = pl.program_id(1)
    n_heads = q_ref.shape[1] // HEAD_DIM
    tile_rows = n_heads * TQ
    lane_head = lax.broadcasted_iota(jnp.int32, (1, MXU_DIM), 1) // HEAD_DIM
    head_masks = [lane_head == hh for hh in range(HEADS_PER_GROUP)]
    group_lanes = lambda head: slice(head // HEADS_PER_GROUP * MXU_DIM, (head // HEADS_PER_GROUP + 1) * MXU_DIM)

    @pl.when(step == 0)
    def _():
        for head in range(n_heads):
            for r0 in range(0, vh_ref.shape[1], TQ):
                vb = v_ref[r0:r0 + TQ, group_lanes(head)]
                vh_ref[head, r0:r0 + TQ, :] = jnp.where(head_masks[head % HEADS_PER_GROUP], vb, jnp.zeros_like(vb))

    for tile in range(TILES_PER_STEP):
        for head in range(n_heads):
            q = q_ref[tile * TQ:(tile + 1) * TQ, group_lanes(head)]
            r0 = tile * tile_rows + head * TQ
            qh_ref[r0:r0 + TQ, :] = jnp.where(head_masks[head % HEADS_PER_GROUP], q, jnp.zeros_like(q))

    key_idx = lax.broadcasted_iota(jnp.int32, (TQ, TQ), 0)
    col_idx = lax.broadcasted_iota(jnp.int32, (TQ, TQ), 1)
    suffix_ones = (key_idx >= col_idx).astype(BF16)
    half = TQ // 2
    below_diag = (lax.broadcasted_iota(jnp.int32, (half, half), 1)
                  < lax.broadcasted_iota(jnp.int32, (half, half), 0))

    def causal_quadrants(fn):
        lo, hi = slice(0, half), slice(half, TQ)
        out = []
        for h0 in range(0, SLAB_ROWS, TQ):
            top, bottom = slice(h0, h0 + half), slice(h0 + half, h0 + TQ)
            upper_left = jnp.where(below_diag, fn(top, lo), 0.0)
            lower_right = jnp.where(below_diag, fn(bottom, hi), 0.0)
            out.append(jnp.concatenate([upper_left, jnp.zeros_like(upper_left)], axis=1))
            out.append(jnp.concatenate([fn(bottom, lo), lower_right], axis=1))
        return jnp.concatenate(out, axis=0)

    acc_ref[...] = jnp.zeros_like(acc_ref)
    carry_ref[...] = jnp.zeros_like(carry_ref)

    def block(tile, k0, diag):
        out_rows = slice(tile * TQ, (tile + 1) * TQ)
        slowest = None
        for r0 in range(0, tile_rows, SLAB_ROWS):
            sl = slice(tile * tile_rows + r0, tile * tile_rows + r0 + SLAB_ROWS)
            lanes = group_lanes(r0 // TQ)
            z = lax.dot_general(qh_ref[sl, :], k_ref[pl.ds(k0, TQ), lanes], (((1,), (1,)), ((), ())),
                                preferred_element_type=F32)
            sp = causal_quadrants(lambda r, c: _softplus2(z[r, c])) if diag else _softplus2(z)
            suffix = jnp.dot(sp.astype(BF16), suffix_ones, preferred_element_type=F32)
            carry = carry_ref[sl, :]
            if diag:
                a = causal_quadrants(lambda r, c: jnp.exp2(z[r, c] - suffix[r, c] + carry[r, :]))
            else:
                a = jnp.exp2(z - suffix + jnp.concatenate([carry] * (TQ // LANES), axis=1))
            a = a.astype(BF16)
            acc = acc_ref[out_rows, lanes]
            for h0 in range(0, SLAB_ROWS, TQ):
                acc += jnp.dot(a[h0:h0 + TQ, :], vh_ref[(r0 + h0) // TQ, pl.ds(k0, TQ), :],
                               preferred_element_type=F32)
            acc_ref[out_rows, lanes] = acc
            carry = carry - jnp.broadcast_to(suffix[:, 0:1], carry.shape)
            carry_ref[sl, :] = carry
            slowest = carry if slowest is None else jnp.maximum(slowest, carry)
        return (jnp.max(slowest) > -EXIT_LOG2).astype(jnp.int32)

    def key_block(tile):
        return TILES_PER_STEP * step + tile

    def key_start(tile, back):
        return pl.multiple_of((key_block(tile) - back) * TQ, TQ)

    @pl.when(step == 0)
    def _():
        live_ref[0] = block(0, key_start(0, 0), True)
        for tile in range(1, TILES_PER_STEP):
            block(tile, key_start(tile, 0), True)
        for tile in range(1, TILES_PER_STEP):
            live_ref[tile] = block(tile, key_start(tile, 1), False)

    @pl.when(step > 0)
    def _():
        for tile in range(TILES_PER_STEP):
            block(tile, key_start(tile, 0), True)
        for tile in range(TILES_PER_STEP):
            live_ref[tile] = block(tile, key_start(tile, 1), False)

    for tile in range(TILES_PER_STEP):
        def cond(state, tile=tile):
            back, live = state
            return jnp.logical_and(back <= key_block(tile), live > 0)

        def body(state, tile=tile):
            back, _ = state
            return back + 1, block(tile, key_start(tile, back), False)

        lax.while_loop(cond, body, (jnp.int32(2), live_ref[tile]))
    o_ref[...] = acc_ref[...].astype(o_ref.dtype)


def _attention(q, k, v, batch, seq):
    n, att_dim = q.shape
    n_heads = att_dim // HEAD_DIM
    steps = seq // (TILES_PER_STEP * TQ)
    qspec = pl.BlockSpec((TILES_PER_STEP * TQ, att_dim), lambda b, i: (b * steps + i, 0))
    kvspec = pl.BlockSpec((seq, att_dim), lambda b, i: (b, 0))
    return pl.pallas_call(
        _attn_kernel,
        grid=(batch, steps),
        in_specs=[qspec, kvspec, kvspec],
        out_specs=qspec,
        out_shape=jax.ShapeDtypeStruct((n, att_dim), BF16),
        scratch_shapes=[pltpu.VMEM((TILES_PER_STEP * TQ, att_dim), F32),
                        pltpu.VMEM((TILES_PER_STEP * n_heads * TQ, LANES), F32),
                        pltpu.VMEM((TILES_PER_STEP * n_heads * TQ, MXU_DIM), BF16),
                        pltpu.VMEM((n_heads, seq, MXU_DIM), BF16),
                        pltpu.SMEM((TILES_PER_STEP,), jnp.int32)],
        compiler_params=pltpu.CompilerParams(
            dimension_semantics=("arbitrary", "arbitrary"), vmem_limit_bytes=VMEM_LIMIT_BYTES),
        name="stickbreak_attn",
    )(q, k, v)


def _mix_ffn_kernel(y_ref, att_ref, gc_ref, ga_ref, x_ref, lng_ref, lnb_ref, wc32_ref, bc_ref, wa32_ref,
                    wo32_ref, gmix_ref, gpre_ref, wup_ref, wdn_ref, gffn_ref, o_ref, wc_ref, wa_ref, wo_ref,
                    *, d_ff, chunk):
    for src_ref, dst_ref in ((wc32_ref, wc_ref), (wa32_ref, wa_ref), (wo32_ref, wo_ref)):
        _cast_once(src_ref, dst_ref)
    att_gated = ga_ref[...].astype(F32) * jnp.dot(att_ref[...], wa_ref[...], preferred_element_type=F32)
    y = y_ref[...]
    yc = y - jnp.mean(y, axis=-1, keepdims=True)
    y = yc * lax.rsqrt(jnp.mean(yc * yc, axis=-1, keepdims=True) + EPS) * lng_ref[...] + lnb_ref[...]
    y = (y * jax.nn.sigmoid(y)).astype(BF16)
    conv_out = jnp.dot(y, wc_ref[...], preferred_element_type=F32) + bc_ref[...]
    merged = gc_ref[...].astype(F32) * conv_out + att_gated
    merged = merged.astype(BF16)
    x1_parts, h_parts = [], []
    for r0 in range(0, TM_FFN, NORM_ROWS):
        rows = slice(r0, r0 + NORM_ROWS)
        mix = jnp.dot(merged[rows, :], wo_ref[...], preferred_element_type=F32)
        x1_parts.append(x_ref[rows, :] + _rms(mix, gmix_ref[...]))
        h_parts.append(_rms(x1_parts[-1], gpre_ref[...]).astype(BF16))
    x1 = jnp.concatenate(x1_parts, axis=0)
    h = jnp.concatenate(h_parts, axis=0)
    ff = None
    for c0 in range(0, d_ff, chunk):
        width = min(chunk, d_ff - c0)
        gate = jnp.dot(h, wup_ref[:, c0:c0 + width], preferred_element_type=F32)
        up = jnp.dot(h, wup_ref[:, d_ff + c0:d_ff + c0 + width], preferred_element_type=F32)
        act = (gate * jax.nn.sigmoid(gate) * up).astype(BF16)
        part = jnp.dot(act, wdn_ref[c0:c0 + width, :], preferred_element_type=F32)
        ff = part if ff is None else ff + part
    o_ref[...] = x1 + _rms(ff, gffn_ref[...])


def _mix_ffn(y, att, gc, ga, x2, lng, lnb, wc, bc, wa, wo, gmix, gpre, wup, wdn, gffn):
    n, d = x2.shape
    row = lambda width: pl.BlockSpec((TM_FFN, width), lambda i: (i, 0))
    consts = (lng, lnb, wc, bc, wa, wo, gmix, gpre, wup, wdn, gffn)
    return pl.pallas_call(
        functools.partial(_mix_ffn_kernel, d_ff=wdn.shape[0], chunk=FFN_CHUNK),
        grid=(n // TM_FFN,),
        in_specs=[row(y.shape[1]), row(att.shape[1]), row(d), row(d), row(d)]
                 + [_const_spec(a.shape) for a in consts],
        out_specs=row(d),
        out_shape=jax.ShapeDtypeStruct((n, d), F32),
        scratch_shapes=[pltpu.VMEM(a.shape, BF16) for a in (wc, wa, wo)],
        compiler_params=pltpu.CompilerParams(
            dimension_semantics=("arbitrary",), vmem_limit_bytes=VMEM_LIMIT_BYTES),
        name="mix_ffn",
    )(y, att, gc, ga, x2, *consts)


def kernel(x, norm_mix_pre, w_in, conv_dw_w, conv_dw_b, conv_ln_g, conv_ln_b, w_conv_branch, b_conv_branch,
           w_att_branch, w_out, norm_mix_post, norm_ffn_pre, w_ffn_up, w_ffn_down, norm_ffn_post):
    b, s, d = x.shape
    conv_dim = conv_dw_w.shape[1]
    att_dim = w_att_branch.shape[0]
    assert att_dim % MXU_DIM == 0 and s % (TILES_PER_STEP * TQ) == 0 and TQ == 2 * LANES
    assert s % TM_PROJ == 0 and (b * s) % TM_FFN == 0
    assert conv_dw_w.shape[0] == CONV_WIDTH and w_in.shape[1] == 2 * conv_dim + 3 * att_dim + 2 * d
    assert HALO >= CONV_WIDTH - 1 - (CONV_WIDTH - 1) % SUBLANES + SUBLANES
    rowvec = lambda a: a.reshape(1, -1).astype(F32)

    x2 = x.reshape(b * s, d)
    y, q, k, v, gc, ga = _inproj(x2, rowvec(norm_mix_pre), w_in.astype(F32), conv_dw_w.astype(F32),
                                 rowvec(conv_dw_b), att_dim, s)
    att = _attention(q, k, v, b, s)
    out = _mix_ffn(y, att, gc, ga, x2, rowvec(conv_ln_g), rowvec(conv_ln_b), w_conv_branch.astype(F32),
                   rowvec(b_conv_branch), w_att_branch.astype(F32), w_out.astype(F32), rowvec(norm_mix_post),
                   rowvec(norm_ffn_pre), w_ffn_up.astype(BF16), w_ffn_down.astype(BF16), rowvec(norm_ffn_post))
    return out.reshape(b, s, d)
```

```python
import functools

import jax
import jax.numpy as jnp
from jax import lax
from jax.experimental import pallas as pl
from jax.experimental.pallas import tpu as pltpu

F32 = jnp.float32
BF16 = jnp.bfloat16

CONV_WIDTH = 31
HEAD_DIM = 64
EPS = 1e-6

SUBLANES = 8
LANES = 128
MXU_DIM = 256
VMEM_LIMIT_BYTES = 56 * 1024 * 1024

TM_PROJ = 512
TM_FFN = 512
TQ = 256
HEADS_PER_GROUP = MXU_DIM // HEAD_DIM
SLAB_ROWS = 2 * TQ
TILES_PER_STEP = 2
HALO = 32
CONV_ROWS = 32
PIECE = 256
FFN_CHUNK = 512
NORM_ROWS = 128
LOG2E = 1.4426950408889634
EXIT_LOG2 = 160.0
SOFTPLUS_CLAMP = 120.0


def _const_spec(shape):
    return pl.BlockSpec(shape, lambda *_: (0,) * len(shape), pipeline_mode=pl.Buffered(1))


def _rms(x, g):
    return x * lax.rsqrt(jnp.mean(x * x, axis=-1, keepdims=True) + EPS) * g


def _exact_zero_like(x):
    bits = lax.bitcast_convert_type(x, jnp.int32)
    return lax.shift_right_logical(lax.shift_right_logical(bits, 16), 16).astype(F32)


def _conv_chunk(ext_ref, taps, bias, r0, lanes):
    rows = CONV_ROWS + SUBLANES
    acc = jnp.broadcast_to(bias, (CONV_ROWS, bias.shape[1]))
    for r in range(SUBLANES):
        part = None
        for d in range(r, CONV_WIDTH, SUBLANES):
            start = r0 + HALO - SUBLANES - (d - r)
            term = taps[CONV_WIDTH - 1 - d] * ext_ref[start:start + rows, lanes]
            part = term if part is None else part + term
        if r:
            part = pltpu.roll(part, r, axis=0)
        acc = acc + part[SUBLANES:, :]
    return acc


def _cast_once(src_ref, dst_ref):
    @pl.when(pl.program_id(0) == 0)
    def _():
        for c0 in range(0, src_ref.shape[1], PIECE):
            dst_ref[:, c0:c0 + PIECE] = src_ref[:, c0:c0 + PIECE].astype(BF16)


def _inproj_kernel(x_ref, g_ref, w32_ref, cw_ref, cb_ref,
                   y_ref, q_ref, k_ref, v_ref, gc_ref, ga_ref, ext_ref, w_ref,
                   *, conv_dim, att_dim, d_model, tiles_per_seq):
    _cast_once(w32_ref, w_ref)
    h = _rms(x_ref[...], g_ref[...]).astype(BF16)

    def proj(c0, width):
        return jnp.dot(h, w_ref[:, c0:c0 + width], preferred_element_type=F32)

    first = (pl.program_id(0) % tiles_per_seq) == 0

    @pl.when(first)
    def _():
        ext_ref[0:HALO, :] = jnp.zeros((HALO, conv_dim), F32)

    @pl.when(jnp.logical_not(first))
    def _():
        ext_ref[0:HALO, :] = ext_ref[TM_PROJ:TM_PROJ + HALO, :]

    def glu_piece(o):
        u = proj(o, PIECE) * jax.nn.sigmoid(proj(conv_dim + o, PIECE))
        ext_ref[HALO:HALO + TM_PROJ, o:o + PIECE] = u
        return _exact_zero_like(u[0:1, 0:1])

    def plain_piece(ref, o, col, scale, gate):
        p = proj(col, PIECE)
        if scale is not None:
            p = p * scale
        if gate:
            p = jax.nn.sigmoid(p)
        ref[:, o:o + PIECE] = p.astype(BF16)
        return _exact_zero_like(p[0:1, 0:1])

    groups = list(range(0, conv_dim, PIECE))
    glu_later = [functools.partial(glu_piece, o) for o in groups[1:]]
    plain = []
    c = 2 * conv_dim
    for ref, scale in ((q_ref, LOG2E * HEAD_DIM ** -0.5), (k_ref, None), (v_ref, None)):
        for o in range(0, att_dim, PIECE):
            plain.append(functools.partial(plain_piece, ref, o, c + o, scale, False))
        c += att_dim
    for ref in (gc_ref, ga_ref):
        for o in range(0, d_model, PIECE):
            plain.append(functools.partial(plain_piece, ref, o, c + o, None, True))
        c += d_model
    chunks = list(range(0, TM_PROJ, CONV_ROWS))
    share = -(-len(plain) // len(groups))
    edge = glu_piece(groups[0])
    for gi, o in enumerate(groups):
        lanes = slice(o, o + PIECE)
        taps = [cw_ref[j:j + 1, lanes] for j in range(CONV_WIDTH)]
        pieces = glu_later[gi:gi + 1] + plain[gi * share:(gi + 1) * share]
        for idx, r0 in enumerate(chunks):
            y_ref[r0:r0 + CONV_ROWS, lanes] = _conv_chunk(ext_ref, taps, cb_ref[:, lanes] + edge, r0, lanes)
            lo, hi = idx * len(pieces) // len(chunks), (idx + 1) * len(pieces) // len(chunks)
            for piece in pieces[lo:hi]:
                edge = piece()


def _inproj(x2, g, w_in, cw, cb, att_dim, seq):
    n, d = x2.shape
    conv_dim = cw.shape[1]
    row = lambda width: pl.BlockSpec((TM_PROJ, width), lambda i: (i, 0))
    return pl.pallas_call(
        functools.partial(_inproj_kernel, conv_dim=conv_dim, att_dim=att_dim, d_model=d,
                          tiles_per_seq=seq // TM_PROJ),
        grid=(n // TM_PROJ,),
        in_specs=[row(d), _const_spec((1, d)), _const_spec(w_in.shape),
                  _const_spec(cw.shape), _const_spec(cb.shape)],
        out_specs=[row(conv_dim), row(att_dim), row(att_dim), row(att_dim), row(d), row(d)],
        scratch_shapes=[pltpu.VMEM((HALO + TM_PROJ, conv_dim), F32), pltpu.VMEM(w_in.shape, BF16)],
        out_shape=[
            jax.ShapeDtypeStruct((n, conv_dim), F32),
            jax.ShapeDtypeStruct((n, att_dim), BF16),
            jax.ShapeDtypeStruct((n, att_dim), BF16),
            jax.ShapeDtypeStruct((n, att_dim), BF16),
            jax.ShapeDtypeStruct((n, d), BF16),
            jax.ShapeDtypeStruct((n, d), BF16),
        ],
        compiler_params=pltpu.CompilerParams(
            dimension_semantics=("arbitrary",), vmem_limit_bytes=VMEM_LIMIT_BYTES),
        name="inproj",
    )(x2, g, w_in, cw, cb)


def _softplus2(z):
    return jnp.maximum(jnp.log2(1.0 + jnp.exp2(jnp.minimum(z, SOFTPLUS_CLAMP))), z)


def _attn_kernel(q_ref, k_ref, v_ref, o_ref, acc_ref, carry_ref, qh_ref, vh_ref, live_ref):
    step = pl.program_id(1)
    n_heads = q_ref.shape[1] // HEAD_DIM
    tile_rows = n_heads * TQ
    lane_head = lax.broadcasted_iota(jnp.int32, (1, MXU_DIM), 1) // HEAD_DIM
    head_masks = [lane_head == hh for hh in range(HEADS_PER_GROUP)]
    group_lanes = lambda head: slice(head // HEADS_PER_GROUP * MXU_DIM, (head // HEADS_PER_GROUP + 1) * MXU_DIM)

    @pl.when(step == 0)
    def _():
        for head in range(n_heads):
            for r0 in range(0, vh_ref.shape[1], TQ):
                vb = v_ref[r0:r0 + TQ, group_lanes(head)]
                vh_ref[head, r0:r0 + TQ, :] = jnp.where(head_masks[head % HEADS_PER_GROUP], vb, jnp.zeros_like(vb))

    for tile in range(TILES_PER_STEP):
        for head in range(n_heads):
            q = q_ref[tile * TQ:(tile + 1) * TQ, group_lanes(head)]
            r0 = tile * tile_rows + head * TQ
            qh_ref[r0:r0 + TQ, :] = jnp.where(head_masks[head % HEADS_PER_GROUP], q, jnp.zeros_like(q))

    key_idx = lax.broadcasted_iota(jnp.int32, (TQ, TQ), 0)
    col_idx = lax.broadcasted_iota(jnp.int32, (TQ, TQ), 1)
    suffix_ones = (key_idx >= col_idx).astype(BF16)
    half = TQ // 2
    below_diag = (lax.broadcasted_iota(jnp.int32, (half, half), 1)
                  < lax.broadcasted_iota(jnp.int32, (half, half), 0))

    def causal_quadrants(fn):
        lo, hi = slice(0, half), slice(half, TQ)
        out = []
        for h0 in range(0, SLAB_ROWS, TQ):
            top, bottom = slice(h0, h0 + half), slice(h0 + half, h0 + TQ)
            upper_left = jnp.where(below_diag, fn(top, lo), 0.0)
            lower_right = jnp.where(below_diag, fn(bottom, hi), 0.0)
            out.append(jnp.concatenate([upper_left, jnp.zeros_like(upper_left)], axis=1))
            out.append(jnp.concatenate([fn(bottom, lo), lower_right], axis=1))
        return jnp.concatenate(out, axis=0)

    acc_ref[...] = jnp.zeros_like(acc_ref)
    carry_ref[...] = jnp.zeros_like(carry_ref)

    def block(tile, k0, diag):
        out_rows = slice(tile * TQ, (tile + 1) * TQ)
        slowest = None
        for r0 in range(0, tile_rows, SLAB_ROWS):
            sl = slice(tile * tile_rows + r0, tile * tile_rows + r0 + SLAB_ROWS)
            lanes = group_lanes(r0 // TQ)
            z = lax.dot_general(qh_ref[sl, :], k_ref[pl.ds(k0, TQ), lanes], (((1,), (1,)), ((), ())),
                                preferred_element_type=F32)
            sp = causal_quadrants(lambda r, c: _softplus2(z[r, c])) if diag else _softplus2(z)
            suffix = jnp.dot(sp.astype(BF16), suffix_ones, preferred_element_type=F32)
            carry = carry_ref[sl, :]
            if diag:
                a = causal_quadrants(lambda r, c: jnp.exp2(z[r, c] - suffix[r, c] + carry[r, :]))
            else:
                a = jnp.exp2(z - suffix + jnp.concatenate([carry] * (TQ // LANES), axis=1))
            a = a.astype(BF16)
            acc = acc_ref[out_rows, lanes]
            for h0 in range(0, SLAB_ROWS, TQ):
                acc += jnp.dot(a[h0:h0 + TQ, :], vh_ref[(r0 + h0) // TQ, pl.ds(k0, TQ), :],
                               preferred_element_type=F32)
            acc_ref[out_rows, lanes] = acc
            carry = carry - jnp.broadcast_to(suffix[:, 0:1], carry.shape)
            carry_ref[sl, :] = carry
            slowest = carry if slowest is None else jnp.maximum(slowest, carry)
        return (jnp.max(slowest) > -EXIT_LOG2).astype(jnp.int32)

    def key_block(tile):
        return TILES_PER_STEP * step + tile

    def key_start(tile, back):
        return pl.multiple_of((key_block(tile) - back) * TQ, TQ)

    @pl.when(step == 0)
    def _():
        live_ref[0] = block(0, key_start(0, 0), True)
        for tile in range(1, TILES_PER_STEP):
            block(tile, key_start(tile, 0), True)
        for tile in range(1, TILES_PER_STEP):
            live_ref[tile] = block(tile, key_start(tile, 1), False)

    @pl.when(step > 0)
    def _():
        for tile in range(TILES_PER_STEP):
            block(tile, key_start(tile, 0), True)
        for tile in range(TILES_PER_STEP):
            live_ref[tile] = block(tile, key_start(tile, 1), False)

    for tile in range(TILES_PER_STEP):
        def cond(state, tile=tile):
            back, live = state
            return jnp.logical_and(back <= key_block(tile), live > 0)

        def body(state, tile=tile):
            back, _ = state
            return back + 1, block(tile, key_start(tile, back), False)

        lax.while_loop(cond, body, (jnp.int32(2), live_ref[tile]))
    o_ref[...] = acc_ref[...].astype(o_ref.dtype)


def _attention(q, k, v, batch, seq):
    n, att_dim = q.shape
    n_heads = att_dim // HEAD_DIM
    steps = seq // (TILES_PER_STEP * TQ)
    qspec = pl.BlockSpec((TILES_PER_STEP * TQ, att_dim), lambda b, i: (b * steps + i, 0))
    kvspec = pl.BlockSpec((seq, att_dim), lambda b, i: (b, 0))
    return pl.pallas_call(
        _attn_kernel,
        grid=(batch, steps),
        in_specs=[qspec, kvspec, kvspec],
        out_specs=qspec,
        out_shape=jax.ShapeDtypeStruct((n, att_dim), BF16),
        scratch_shapes=[pltpu.VMEM((TILES_PER_STEP * TQ, att_dim), F32),
                        pltpu.VMEM((TILES_PER_STEP * n_heads * TQ, LANES), F32),
                        pltpu.VMEM((TILES_PER_STEP * n_heads * TQ, MXU_DIM), BF16),
                        pltpu.VMEM((n_heads, seq, MXU_DIM), BF16),
                        pltpu.SMEM((TILES_PER_STEP,), jnp.int32)],
        compiler_params=pltpu.CompilerParams(
            dimension_semantics=("arbitrary", "arbitrary"), vmem_limit_bytes=VMEM_LIMIT_BYTES),
        name="stickbreak_attn",
    )(q, k, v)


def _mix_ffn_kernel(y_ref, att_ref, gc_ref, ga_ref, x_ref, lng_ref, lnb_ref, wc32_ref, bc_ref, wa32_ref,
                    wo32_ref, gmix_ref, gpre_ref, wup_ref, wdn_ref, gffn_ref, o_ref, wc_ref, wa_ref, wo_ref,
                    *, d_ff, chunk):
    for src_ref, dst_ref in ((wc32_ref, wc_ref), (wa32_ref, wa_ref), (wo32_ref, wo_ref)):
        _cast_once(src_ref, dst_ref)
    att_gated = ga_ref[...].astype(F32) * jnp.dot(att_ref[...], wa_ref[...], preferred_element_type=F32)
    y = y_ref[...]
    yc = y - jnp.mean(y, axis=-1, keepdims=True)
    y = yc * lax.rsqrt(jnp.mean(yc * yc, axis=-1, keepdims=True) + EPS) * lng_ref[...] + lnb_ref[...]
    y = (y * jax.nn.sigmoid(y)).astype(BF16)
    conv_out = jnp.dot(y, wc_ref[...], preferred_element_type=F32) + bc_ref[...]
    merged = gc_ref[...].astype(F32) * conv_out + att_gated
    merged = merged.astype(BF16)
    x1_parts, h_parts = [], []
    for r0 in range(0, TM_FFN, NORM_ROWS):
        rows = slice(r0, r0 + NORM_ROWS)
        mix = jnp.dot(merged[rows, :], wo_ref[...], preferred_element_type=F32)
        x1_parts.append(x_ref[rows, :] + _rms(mix, gmix_ref[...]))
        h_parts.append(_rms(x1_parts[-1], gpre_ref[...]).astype(BF16))
    x1 = jnp.concatenate(x1_parts, axis=0)
    h = jnp.concatenate(h_parts, axis=0)
    ff = None
    for c0 in range(0, d_ff, chunk):
        width = min(chunk, d_ff - c0)
        gate = jnp.dot(h, wup_ref[:, c0:c0 + width], preferred_element_type=F32)
        up = jnp.dot(h, wup_ref[:, d_ff + c0:d_ff + c0 + width], preferred_element_type=F32)
        act = (gate * jax.nn.sigmoid(gate) * up).astype(BF16)
        part = jnp.dot(act, wdn_ref[c0:c0 + width, :], preferred_element_type=F32)
        ff = part if ff is None else ff + part
    o_ref[...] = x1 + _rms(ff, gffn_ref[...])


def _mix_ffn(y, att, gc, ga, x2, lng, lnb, wc, bc, wa, wo, gmix, gpre, wup, wdn, gffn):
    n, d = x2.shape
    row = lambda width: pl.BlockSpec((TM_FFN, width), lambda i: (i, 0))
    consts = (lng, lnb, wc, bc, wa, wo, gmix, gpre, wup, wdn, gffn)
    return pl.pallas_call(
        functools.partial(_mix_ffn_kernel, d_ff=wdn.shape[0], chunk=FFN_CHUNK),
        grid=(n // TM_FFN,),
        in_specs=[row(y.shape[1]), row(att.shape[1]), row(d), row(d), row(d)]
                 + [_const_spec(a.shape) for a in consts],
        out_specs=row(d),
        out_shape=jax.ShapeDtypeStruct((n, d), F32),
        scratch_shapes=[pltpu.VMEM(a.shape, BF16) for a in (wc, wa, wo)],
        compiler_params=pltpu.CompilerParams(
            dimension_semantics=("arbitrary",), vmem_limit_bytes=VMEM_LIMIT_BYTES),
        name="mix_ffn",
    )(y, att, gc, ga, x2, *consts)


def kernel(x, norm_mix_pre, w_in, conv_dw_w, conv_dw_b, conv_ln_g, conv_ln_b, w_conv_branch, b_conv_branch,
           w_att_branch, w_out, norm_mix_post, norm_ffn_pre, w_ffn_up, w_ffn_down, norm_ffn_post):
    b, s, d = x.shape
    conv_dim = conv_dw_w.shape[1]
    att_dim = w_att_branch.shape[0]
    assert att_dim % MXU_DIM == 0 and s % (TILES_PER_STEP * TQ) == 0 and TQ == 2 * LANES
    assert s % TM_PROJ == 0 and (b * s) % TM_FFN == 0
    assert conv_dw_w.shape[0] == CONV_WIDTH and w_in.shape[1] == 2 * conv_dim + 3 * att_dim + 2 * d
    assert HALO >= CONV_WIDTH - 1 - (CONV_WIDTH - 1) % SUBLANES + SUBLANES
    rowvec = lambda a: a.reshape(1, -1).astype(F32)

    x2 = x.reshape(b * s, d)
    y, q, k, v, gc, ga = _inproj(x2, rowvec(norm_mix_pre), w_in.astype(F32), conv_dw_w.astype(F32),
                                 rowvec(conv_dw_b), att_dim, s)
    att = _attention(q, k, v, b, s)
    out = _mix_ffn(y, att, gc, ga, x2, rowvec(conv_ln_g), rowvec(conv_ln_b), w_conv_branch.astype(F32),
                   rowvec(b_conv_branch), w_att_branch.astype(F32), w_out.astype(F32), rowvec(norm_mix_post),
                   rowvec(norm_ffn_pre), w_ffn_up.astype(BF16), w_ffn_down.astype(BF16), rowvec(norm_ffn_post))
    return out.reshape(b, s, d)
```

```python
import functools

import jax
import jax.numpy as jnp
from jax import lax
from jax.experimental import pallas as pl
from jax.experimental.pallas import tpu as pltpu

F32 = jnp.float32
BF16 = jnp.bfloat16

CONV_WIDTH = 31
HEAD_DIM = 64
EPS = 1e-6

SUBLANES = 8
LANES = 128
MXU_DIM = 256
VMEM_LIMIT_BYTES = 56 * 1024 * 1024

TM_PROJ = 512
TM_FFN = 512
TQ = 256
HEADS_PER_GROUP = MXU_DIM // HEAD_DIM
SLAB_ROWS = 2 * TQ
TILES_PER_STEP = 2
HALO = 32
CONV_ROWS = 32
PIECE = 256
FFN_CHUNK = 512
NORM_ROWS = 128
LOG2E = 1.4426950408889634
EXIT_LOG2 = 160.0
SOFTPLUS_CLAMP = 120.0


def _const_spec(shape):
    return pl.BlockSpec(shape, lambda *_: (0,) * len(shape), pipeline_mode=pl.Buffered(1))


def _rms(x, g):
    return x * lax.rsqrt(jnp.mean(x * x, axis=-1, keepdims=True) + EPS) * g


def _exact_zero_like(x):
    bits = lax.bitcast_convert_type(x, jnp.int32)
    return lax.shift_right_logical(lax.shift_right_logical(bits, 16), 16).astype(F32)


def _conv_chunk(ext_ref, taps, bias, r0, lanes):
    rows = CONV_ROWS + SUBLANES
    acc = jnp.broadcast_to(bias, (CONV_ROWS, bias.shape[1]))
    for r in range(SUBLANES):
        part = None
        for d in range(r, CONV_WIDTH, SUBLANES):
            start = r0 + HALO - SUBLANES - (d - r)
            term = taps[CONV_WIDTH - 1 - d] * ext_ref[start:start + rows, lanes]
            part = term if part is None else part + term
        if r:
            part = pltpu.roll(part, r, axis=0)
        acc = acc + part[SUBLANES:, :]
    return acc


def _cast_once(src_ref, dst_ref):
    @pl.when(pl.program_id(0) == 0)
    def _():
        for c0 in range(0, src_ref.shape[1], PIECE):
            dst_ref[:, c0:c0 + PIECE] = src_ref[:, c0:c0 + PIECE].astype(BF16)


def _inproj_kernel(x_ref, g_ref, w32_ref, cw_ref, cb_ref,
                   y_ref, q_ref, k_ref, v_ref, gc_ref, ga_ref, ext_ref, w_ref,
                   *, conv_dim, att_dim, d_model, tiles_per_seq):
    _cast_once(w32_ref, w_ref)
    h = _rms(x_ref[...], g_ref[...]).astype(BF16)

    def proj(c0, width):
        return jnp.dot(h, w_ref[:, c0:c0 + width], preferred_element_type=F32)

    first = (pl.program_id(0) % tiles_per_seq) == 0

    @pl.when(first)
    def _():
        ext_ref[0:HALO, :] = jnp.zeros((HALO, conv_dim), F32)

    @pl.when(jnp.logical_not(first))
    def _():
        ext_ref[0:HALO, :] = ext_ref[TM_PROJ:TM_PROJ + HALO, :]

    def glu_piece(o):
        u = proj(o, PIECE) * jax.nn.sigmoid(proj(conv_dim + o, PIECE))
        ext_ref[HALO:HALO + TM_PROJ, o:o + PIECE] = u
        return _exact_zero_like(u[0:1, 0:1])

    def plain_piece(ref, o, col, scale, gate):
        p = proj(col, PIECE)
        if scale is not None:
            p = p * scale
        if gate:
            p = jax.nn.sigmoid(p)
        ref[:, o:o + PIECE] = p.astype(BF16)
        return _exact_zero_like(p[0:1, 0:1])

    groups = list(range(0, conv_dim, PIECE))
    glu_later = [functools.partial(glu_piece, o) for o in groups[1:]]
    plain = []
    c = 2 * conv_dim
    for ref, scale in ((q_ref, LOG2E * HEAD_DIM ** -0.5), (k_ref, None), (v_ref, None)):
        for o in range(0, att_dim, PIECE):
            plain.append(functools.partial(plain_piece, ref, o, c + o, scale, False))
        c += att_dim
    for ref in (gc_ref, ga_ref):
        for o in range(0, d_model, PIECE):
            plain.append(functools.partial(plain_piece, ref, o, c + o, None, True))
        c += d_model
    chunks = list(range(0, TM_PROJ, CONV_ROWS))
    share = -(-len(plain) // len(groups))
    edge = glu_piece(groups[0])
    for gi, o in enumerate(groups):
        lanes = slice(o, o + PIECE)
        taps = [cw_ref[j:j + 1, lanes] for j in range(CONV_WIDTH)]
        pieces = glu_later[gi:gi + 1] + plain[gi * share:(gi + 1) * share]
        for idx, r0 in enumerate(chunks):
            y_ref[r0:r0 + CONV_ROWS, lanes] = _conv_chunk(ext_ref, taps, cb_ref[:, lanes] + edge, r0, lanes)
            lo, hi = idx * len(pieces) // len(chunks), (idx + 1) * len(pieces) // len(chunks)
            for piece in pieces[lo:hi]:
                edge = piece()


def _inproj(x2, g, w_in, cw, cb, att_dim, seq):
    n, d = x2.shape
    conv_dim = cw.shape[1]
    row = lambda width: pl.BlockSpec((TM_PROJ, width), lambda i: (i, 0))
    return pl.pallas_call(
        functools.partial(_inproj_kernel, conv_dim=conv_dim, att_dim=att_dim, d_model=d,
                          tiles_per_seq=seq // TM_PROJ),
        grid=(n // TM_PROJ,),
        in_specs=[row(d), _const_spec((1, d)), _const_spec(w_in.shape),
                  _const_spec(cw.shape), _const_spec(cb.shape)],
        out_specs=[row(conv_dim), row(att_dim), row(att_dim), row(att_dim), row(d), row(d)],
        scratch_shapes=[pltpu.VMEM((HALO + TM_PROJ, conv_dim), F32), pltpu.VMEM(w_in.shape, BF16)],
        out_shape=[
            jax.ShapeDtypeStruct((n, conv_dim), F32),
            jax.ShapeDtypeStruct((n, att_dim), BF16),
            jax.ShapeDtypeStruct((n, att_dim), BF16),
            jax.ShapeDtypeStruct((n, att_dim), BF16),
            jax.ShapeDtypeStruct((n, d), BF16),
            jax.ShapeDtypeStruct((n, d), BF16),
        ],
        compiler_params=pltpu.CompilerParams(
            dimension_semantics=("arbitrary",), vmem_limit_bytes=VMEM_LIMIT_BYTES),
        name="inproj",
    )(x2, g, w_in, cw, cb)


def _softplus2(z):
    return jnp.maximum(jnp.log2(1.0 + jnp.exp2(jnp.minimum(z, SOFTPLUS_CLAMP))), z)


def _attn_kernel(q_ref, k_ref, v_ref, o_ref, acc_ref, carry_ref, qh_ref, vh_ref, live_ref):
    step = pl.program_id(1)
    n_heads = q_ref.shape[1] // HEAD_DIM
    tile_rows = n_heads * TQ
    lane_head = lax.broadcasted_iota(jnp.int32, (1, MXU_DIM), 1) // HEAD_DIM
    head_masks = [lane_head == hh for hh in range(HEADS_PER_GROUP)]
    group_lanes = lambda head: slice(head // HEADS_PER_GROUP * MXU_DIM, (head // HEADS_PER_GROUP + 1) * MXU_DIM)

    @pl.when(step == 0)
    def _():
        for head in range(n_heads):
            for r0 in range(0, vh_ref.shape[1], TQ):
                vb = v_ref[r0:r0 + TQ, group_lanes(head)]
                vh_ref[head, r0:r0 + TQ, :] = jnp.where(head_masks[head % HEADS_PER_GROUP], vb, jnp.zeros_like(vb))

    for tile in range(TILES_PER_STEP):
        for head in range(n_heads):
            q = q_ref[tile * TQ:(tile + 1) * TQ, group_lanes(head)]
            r0 = tile * tile_rows + head * TQ
            qh_ref[r0:r0 + TQ, :] = jnp.where(head_masks[head % HEADS_PER_GROUP], q, jnp.zeros_like(q))

    key_idx = lax.broadcasted_iota(jnp.int32, (TQ, TQ), 0)
    col_idx = lax.broadcasted_iota(jnp.int32, (TQ, TQ), 1)
    suffix_ones = (key_idx >= col_idx).astype(BF16)
    half = TQ // 2
    below_diag = (lax.broadcasted_iota(jnp.int32, (half, half), 1)
                  < lax.broadcasted_iota(jnp.int32, (half, half), 0))

    def causal_quadrants(fn):
        lo, hi = slice(0, half), slice(half, TQ)
        out = []
        for h0 in range(0, SLAB_ROWS, TQ):
            top, bottom = slice(h0, h0 + half), slice(h0 + half, h0 + TQ)
            upper_left = jnp.where(below_diag, fn(top, lo), 0.0)
            lower_right = jnp.where(below_diag, fn(bottom, hi), 0.0)
            out.append(jnp.concatenate([upper_left, jnp.zeros_like(upper_left)], axis=1))
            out.append(jnp.concatenate([fn(bottom, lo), lower_right], axis=1))
        return jnp.concatenate(out, axis=0)

    acc_ref[...] = jnp.zeros_like(acc_ref)
    carry_ref[...] = jnp.zeros_like(carry_ref)

    def block(tile, k0, diag):
        out_rows = slice(tile * TQ, (tile + 1) * TQ)
        slowest = None
        for r0 in range(0, tile_rows, SLAB_ROWS):
            sl = slice(tile * tile_rows + r0, tile * tile_rows + r0 + SLAB_ROWS)
            lanes = group_lanes(r0 // TQ)
            z = lax.dot_general(qh_ref[sl, :], k_ref[pl.ds(k0, TQ), lanes], (((1,), (1,)), ((), ())),
                                preferred_element_type=F32)
            sp = causal_quadrants(lambda r, c: _softplus2(z[r, c])) if diag else _softplus2(z)
            suffix = jnp.dot(sp.astype(BF16), suffix_ones, preferred_element_type=F32)
            carry = carry_ref[sl, :]
            if diag:
                a = causal_quadrants(lambda r, c: jnp.exp2(z[r, c] - suffix[r, c] + carry[r, :]))
            else:
                a = jnp.exp2(z - suffix + jnp.concatenate([carry] * (TQ // LANES), axis=1))
            a = a.astype(BF16)
            acc = acc_ref[out_rows, lanes]
            for h0 in range(0, SLAB_ROWS, TQ):
                acc += jnp.dot(a[h0:h0 + TQ, :], vh_ref[(r0 + h0) // TQ, pl.ds(k0, TQ), :],
                               preferred_element_type=F32)
            acc_ref[out_rows, lanes] = acc
            carry = carry - jnp.broadcast_to(suffix[:, 0:1], carry.shape)
            carry_ref[sl, :] = carry
            slowest = carry if slowest is None else jnp.maximum(slowest, carry)
        return (jnp.max(slowest) > -EXIT_LOG2).astype(jnp.int32)

    def key_block(tile):
        return TILES_PER_STEP * step + tile

    def key_start(tile, back):
        return pl.multiple_of((key_block(tile) - back) * TQ, TQ)

    @pl.when(step == 0)
    def _():
        live_ref[0] = block(0, key_start(0, 0), True)
        for tile in range(1, TILES_PER_STEP):
            block(tile, key_start(tile, 0), True)
        for tile in range(1, TILES_PER_STEP):
            live_ref[tile] = block(tile, key_start(tile, 1), False)

    @pl.when(step > 0)
    def _():
        for tile in range(TILES_PER_STEP):
            block(tile, key_start(tile, 0), True)
        for tile in range(TILES_PER_STEP):
            live_ref[tile] = block(tile, key_start(tile, 1), False)

    for tile in range(TILES_PER_STEP):
        def cond(state, tile=tile):
            back, live = state
            return jnp.logical_and(back <= key_block(tile), live > 0)

        def body(state, tile=tile):
            back, _ = state
            return back + 1, block(tile, key_start(tile, back), False)

        lax.while_loop(cond, body, (jnp.int32(2), live_ref[tile]))
    o_ref[...] = acc_ref[...].astype(o_ref.dtype)


def _attention(q, k, v, batch, seq):
    n, att_dim = q.shape
    n_heads = att_dim // HEAD_DIM
    steps = seq // (TILES_PER_STEP * TQ)
    qspec = pl.BlockSpec((TILES_PER_STEP * TQ, att_dim), lambda b, i: (b * steps + i, 0))
    kvspec = pl.BlockSpec((seq, att_dim), lambda b, i: (b, 0))
    return pl.pallas_call(
        _attn_kernel,
        grid=(batch, steps),
        in_specs=[qspec, kvspec, kvspec],
        out_specs=qspec,
        out_shape=jax.ShapeDtypeStruct((n, att_dim), BF16),
        scratch_shapes=[pltpu.VMEM((TILES_PER_STEP * TQ, att_dim), F32),
                        pltpu.VMEM((TILES_PER_STEP * n_heads * TQ, LANES), F32),
                        pltpu.VMEM((TILES_PER_STEP * n_heads * TQ, MXU_DIM), BF16),
                        pltpu.VMEM((n_heads, seq, MXU_DIM), BF16),
                        pltpu.SMEM((TILES_PER_STEP,), jnp.int32)],
        compiler_params=pltpu.CompilerParams(
            dimension_semantics=("arbitrary", "arbitrary"), vmem_limit_bytes=VMEM_LIMIT_BYTES),
        name="stickbreak_attn",
    )(q, k, v)


def _mix_ffn_kernel(y_ref, att_ref, gc_ref, ga_ref, x_ref, wup32_ref, wdn32_ref, lng_ref, lnb_ref, wc32_ref,
                    bc_ref, wa32_ref, wo32_ref, gmix_ref, gpre_ref, gffn_ref, o_ref, wc_ref, wa_ref, wo_ref,
                    wup_ref, wdn_ref, *, d_ff, chunk, n_cast):
    step = pl.program_id(0)
    for src_ref, dst_ref in ((wc32_ref, wc_ref), (wa32_ref, wa_ref), (wo32_ref, wo_ref)):
        _cast_once(src_ref, dst_ref)
    up_cols, dn_rows = wup32_ref.shape[1], wdn32_ref.shape[0]
    for c in range(n_cast):
        @pl.when(step == c)
        def _(c=c):
            wup_ref[:, c * up_cols:(c + 1) * up_cols] = wup32_ref[...].astype(BF16)
            wdn_ref[c * dn_rows:(c + 1) * dn_rows, :] = wdn32_ref[...].astype(BF16)

    pl.when(step >= n_cast)(functools.partial(
        _mix_ffn_tile, y_ref, att_ref, gc_ref, ga_ref, x_ref, lng_ref, lnb_ref, wc_ref, bc_ref, wa_ref, wo_ref,
        gmix_ref, gpre_ref, wup_ref, wdn_ref, gffn_ref, o_ref, d_ff=d_ff, chunk=chunk))


def _mix_ffn_tile(y_ref, att_ref, gc_ref, ga_ref, x_ref, lng_ref, lnb_ref, wc_ref, bc_ref, wa_ref, wo_ref,
                  gmix_ref, gpre_ref, wup_ref, wdn_ref, gffn_ref, o_ref, *, d_ff, chunk):
    att_gated = ga_ref[...].astype(F32) * jnp.dot(att_ref[...], wa_ref[...], preferred_element_type=F32)
    y = y_ref[...]
    yc = y - jnp.mean(y, axis=-1, keepdims=True)
    y = yc * lax.rsqrt(jnp.mean(yc * yc, axis=-1, keepdims=True) + EPS) * lng_ref[...] + lnb_ref[...]
    y = (y * jax.nn.sigmoid(y)).astype(BF16)
    conv_out = jnp.dot(y, wc_ref[...], preferred_element_type=F32) + bc_ref[...]
    merged = gc_ref[...].astype(F32) * conv_out + att_gated
    merged = merged.astype(BF16)
    x1_parts, h_parts = [], []
    for r0 in range(0, TM_FFN, NORM_ROWS):
        rows = slice(r0, r0 + NORM_ROWS)
        mix = jnp.dot(merged[rows, :], wo_ref[...], preferred_element_type=F32)
        x1_parts.append(x_ref[rows, :] + _rms(mix, gmix_ref[...]))
        h_parts.append(_rms(x1_parts[-1], gpre_ref[...]).astype(BF16))
    x1 = jnp.concatenate(x1_parts, axis=0)
    h = jnp.concatenate(h_parts, axis=0)
    ff = None
    for c0 in range(0, d_ff, chunk):
        width = min(chunk, d_ff - c0)
        gate = jnp.dot(h, wup_ref[:, c0:c0 + width], preferred_element_type=F32)
        up = jnp.dot(h, wup_ref[:, d_ff + c0:d_ff + c0 + width], preferred_element_type=F32)
        act = (gate * jax.nn.sigmoid(gate) * up).astype(BF16)
        part = jnp.dot(act, wdn_ref[c0:c0 + width, :], preferred_element_type=F32)
        ff = part if ff is None else ff + part
    o_ref[...] = x1 + _rms(ff, gffn_ref[...])


def _mix_ffn(y, att, gc, ga, x2, lng, lnb, wc, bc, wa, wo, gmix, gpre, wup, wdn, gffn):
    n, d = x2.shape
    d_ff = wdn.shape[0]
    n_cast = d_ff // MXU_DIM
    last = n_cast - 1
    row = lambda width: pl.BlockSpec((TM_FFN, width), lambda s: (jnp.maximum(s - n_cast, 0), 0))
    up_slab = pl.BlockSpec((d, wup.shape[1] // n_cast), lambda s: (0, jnp.minimum(s, last)))
    dn_slab = pl.BlockSpec((d_ff // n_cast, d), lambda s: (jnp.minimum(s, last), 0))
    consts = (lng, lnb, wc, bc, wa, wo, gmix, gpre, gffn)
    return pl.pallas_call(
        functools.partial(_mix_ffn_kernel, d_ff=d_ff, chunk=FFN_CHUNK, n_cast=n_cast),
        grid=(n_cast + n // TM_FFN,),
        in_specs=[row(y.shape[1]), row(att.shape[1]), row(d), row(d), row(d), up_slab, dn_slab]
                 + [_const_spec(a.shape) for a in consts],
        out_specs=row(d),
        out_shape=jax.ShapeDtypeStruct((n, d), F32),
        scratch_shapes=[pltpu.VMEM(a.shape, BF16) for a in (wc, wa, wo, wup, wdn)],
        compiler_params=pltpu.CompilerParams(
            dimension_semantics=("arbitrary",), vmem_limit_bytes=VMEM_LIMIT_BYTES),
        name="mix_ffn",
    )(y, att, gc, ga, x2, wup, wdn, *consts)


def kernel(x, norm_mix_pre, w_in, conv_dw_w, conv_dw_b, conv_ln_g, conv_ln_b, w_conv_branch, b_conv_branch,
           w_att_branch, w_out, norm_mix_post, norm_ffn_pre, w_ffn_up, w_ffn_down, norm_ffn_post):
    b, s, d = x.shape
    conv_dim = conv_dw_w.shape[1]
    att_dim = w_att_branch.shape[0]
    assert att_dim % MXU_DIM == 0 and s % (TILES_PER_STEP * TQ) == 0 and TQ == 2 * LANES
    assert s % TM_PROJ == 0 and (b * s) % TM_FFN == 0
    assert conv_dw_w.shape[0] == CONV_WIDTH and w_in.shape[1] == 2 * conv_dim + 3 * att_dim + 2 * d
    assert HALO >= CONV_WIDTH - 1 - (CONV_WIDTH - 1) % SUBLANES + SUBLANES
    rowvec = lambda a: a.reshape(1, -1).astype(F32)

    x2 = x.reshape(b * s, d)
    y, q, k, v, gc, ga = _inproj(x2, rowvec(norm_mix_pre), w_in.astype(F32), conv_dw_w.astype(F32),
                                 rowvec(conv_dw_b), att_dim, s)
    att = _attention(q, k, v, b, s)
    out = _mix_ffn(y, att, gc, ga, x2, rowvec(conv_ln_g), rowvec(conv_ln_b), w_conv_branch.astype(F32),
                   rowvec(b_conv_branch), w_att_branch.astype(F32), w_out.astype(F32), rowvec(norm_mix_post),
                   rowvec(norm_ffn_pre), w_ffn_up.astype(F32), w_ffn_down.astype(F32), rowvec(norm_ffn_post))
    return out.reshape(b, s, d)
```
